```python
import math
import jax, jax.numpy as jnp
from jax import lax
import numpy as np

D_MODEL = 2048
BATCH = 4
SEQ = 2048
DEPTH = 2
DEC_BATCH = 128
DEC_SEQ = 1
PAST_LEN = 2048
PAGE_SIZE = 128

N_HEADS_A = 8
HEAD_DIM = 128
A_WIDTH = N_HEADS_A * 2 * HEAD_DIM
ROPE_THETA = 10000.0
Q_BLOCK = 128
C_R = D_MODEL // 2
RWKV_HEAD = 64
H_R = C_R // RWKV_HEAD
DECAY_RANK = 64
A_RANK = 64
G_RANK = 32
RWKV_IN = 3 * C_R + DECAY_RANK + A_RANK + G_RANK
C_C = D_MODEL // 2
CONV_W = 31
N_BRANCH = 3
D_FF = 4 * D_MODEL
O_Q = 0
O_K = O_Q + A_WIDTH
O_V = O_K + A_WIDTH
O_R = O_V + A_WIDTH
O_C = O_R + RWKV_IN
O_G = O_C + 2 * C_C
N_IN = O_G + N_BRANCH * D_MODEL
RMS_EPS = 1e-6
LN_EPS = 1e-5
LN_X_EPS = 64e-5

kernel_name = 'hybrid_diffattn_rwkv7_conformer_step'


def rms_norm(x, g, eps=RMS_EPS):
    xf = x.astype(jnp.float32)
    y = xf * lax.rsqrt(jnp.mean(xf * xf, axis=-1, keepdims=True) + eps)
    return (y * g.astype(jnp.float32)).astype(x.dtype)


def layer_norm(x, g, b, eps):
    xf = x.astype(jnp.float32)
    mu = jnp.mean(xf, axis=-1, keepdims=True)
    var = jnp.mean(jnp.square(xf - mu), axis=-1, keepdims=True)
    return (xf - mu) * lax.rsqrt(var + eps) * g.astype(jnp.float32) + b.astype(jnp.float32)


def rope(x, pos):
    half = HEAD_DIM // 2
    freqs = ROPE_THETA ** (-jnp.arange(half, dtype=jnp.float32) * (2.0 / HEAD_DIM))
    ang = pos.astype(jnp.float32)[:, None] * freqs[None, :]
    cos = jnp.cos(ang)[None, :, None, None, :]
    sin = jnp.sin(ang)[None, :, None, None, :]
    x1, x2 = x[..., :half], x[..., half:]
    return jnp.concatenate([x1 * cos - x2 * sin, x2 * cos + x1 * sin], axis=-1)


def diff_attend(q, k, v, q_pos, k_pos, lam):
    s = jnp.einsum('bqhmd,bkhmd->bhmqk', q, k) * (HEAD_DIM ** -0.5)
    mask = k_pos[None, :] <= q_pos[:, None]
    s = jnp.where(mask, s, jnp.finfo(jnp.float32).min)
    p = jax.nn.softmax(s, axis=-1)
    a = p[:, :, 0] - lam * p[:, :, 1]
    return jnp.einsum('bhqk,bkhe->bqhe', a, v)


def attend(q, k, v, q_pos, k_pos, lam):
    b, tq = q.shape[0], q.shape[1]
    if tq <= Q_BLOCK or tq % Q_BLOCK != 0:
        return diff_attend(q, k, v, q_pos, k_pos, lam)
    nb = tq // Q_BLOCK
    qb = jnp.moveaxis(q.reshape(b, nb, Q_BLOCK, N_HEADS_A, 2, HEAD_DIM), 1, 0)
    pb = q_pos.reshape(nb, Q_BLOCK)
    ob = lax.map(lambda qp: diff_attend(qp[0], k, v, qp[1], k_pos, lam), (qb, pb))
    return jnp.moveaxis(ob, 0, 1).reshape(b, tq, N_HEADS_A, 2 * HEAD_DIM)


def rwkv_scan(s0, r, w, k, v, kk, a):
    def step(S, inp):
        r_t, w_t, k_t, v_t, kk_t, a_t = inp
        sk = jnp.einsum('bhvk,bhk->bhv', S, -kk_t)
        S = S * w_t[:, :, None, :] + sk[..., None] * (kk_t * a_t)[:, :, None, :] + v_t[..., None] * k_t[:, :, None, :]
        y = jnp.einsum('bhvk,bhk->bhv', S, r_t)
        return S, y
    xs = tuple(jnp.swapaxes(t, 0, 1) for t in (r, w, k, v, kk, a))
    s_new, ys = lax.scan(step, s0, xs)
    return jnp.swapaxes(ys, 0, 1), s_new


def layer(x, pos, past_k, past_v, s0, shift_prev, conv_prev, p, lambda_init):
    f32 = jnp.float32
    B, T, _ = x.shape
    h = rms_norm(x, p['norm1_g'])
    proj = jnp.einsum('btd,dn->btn', h, p['w_in'])

    qh = rope(rms_norm(proj[..., O_Q:O_K].reshape(B, T, N_HEADS_A, 2, HEAD_DIM).astype(f32), p['q_norm_g']), pos)
    kh = rope(rms_norm(proj[..., O_K:O_V].reshape(B, T, N_HEADS_A, 2, HEAD_DIM).astype(f32), p['k_norm_g']), pos)
    vh = proj[..., O_V:O_R].reshape(B, T, N_HEADS_A, 2 * HEAD_DIM)
    if past_k is None:
        keys, vals = kh, vh.astype(f32)
    else:
        keys = jnp.concatenate([past_k.astype(f32), kh], axis=1)
        vals = jnp.concatenate([past_v.astype(f32), vh.astype(f32)], axis=1)
    k_pos = jnp.arange(keys.shape[1])
    lam = (jnp.exp(jnp.sum(p['lambda_q1'].astype(f32) * p['lambda_k1'].astype(f32)))
           - jnp.exp(jnp.sum(p['lambda_q2'].astype(f32) * p['lambda_k2'].astype(f32))) + lambda_init)
    o = attend(qh, keys, vals, pos, k_pos, lam)
    o_a = (rms_norm(o, p['subln_g'], LN_EPS) * (1.0 - lambda_init)).reshape(B, T, A_WIDTH).astype(x.dtype)

    pr = proj[..., O_R:O_C]
    prev = jnp.concatenate([shift_prev[:, None].astype(pr.dtype), pr[:, :-1]], axis=1)
    ps = (pr + p['rwkv_mu'] * (prev - pr)).astype(f32)
    new_shift = pr[:, -1]
    r = ps[..., 0:C_R]
    kr = ps[..., C_R:2 * C_R]
    vr = ps[..., 2 * C_R:3 * C_R]
    wd = ps[..., 3 * C_R:3 * C_R + DECAY_RANK]
    ad = ps[..., 3 * C_R + DECAY_RANK:3 * C_R + DECAY_RANK + A_RANK]
    gd = ps[..., 3 * C_R + DECAY_RANK + A_RANK:]
    w_log = -jax.nn.softplus(-(p['rwkv_w0'] + jnp.tanh(wd) @ p['rwkv_w2'])) - 0.5
    decay = jnp.exp(-jnp.exp(w_log))
    a = jax.nn.sigmoid(p['rwkv_a0'] + ad @ p['rwkv_a2'])
    g = jax.nn.sigmoid(gd) @ p['rwkv_g2']
    heads = lambda t: t.reshape(B, T, H_R, RWKV_HEAD)
    kk = heads(kr * p['rwkv_kk'])
    kk = kk * lax.rsqrt(jnp.sum(kk * kk, axis=-1, keepdims=True) + 1e-12)
    kr = kr * (1.0 + (a - 1.0) * p['rwkv_ka'])
    y, s_new = rwkv_scan(s0, heads(r), heads(decay), heads(kr), heads(vr), kk, heads(a))
    mu = jnp.mean(y, axis=-1, keepdims=True)
    var = jnp.mean(jnp.square(y - mu), axis=-1, keepdims=True)
    yn = ((y - mu) * lax.rsqrt(var + LN_X_EPS)).reshape(B, T, C_R) * p['rwkv_lnx_g'] + p['rwkv_lnx_b']
    bonus = jnp.sum(heads(r) * heads(kr) * p['rwkv_rk'], axis=-1, keepdims=True) * heads(vr)
    o_b = ((yn + bonus.reshape(B, T, C_R)) * g).astype(x.dtype)

    cu = proj[..., O_C:O_G]
    u = cu[..., :C_C] * jax.nn.sigmoid(cu[..., C_C:])
    full = jnp.concatenate([conv_prev.astype(u.dtype), u], axis=1)
    yc = lax.conv_general_dilated(full, p['conv_w'][:, None, :].astype(full.dtype), window_strides=(1,),
                                  padding='VALID', dimension_numbers=('NWC', 'WIO', 'NWC'),
                                  feature_group_count=C_C) + p['conv_b']
    new_conv = full[:, -(CONV_W - 1):]
    o_c = jax.nn.silu(layer_norm(yc, p['conv_ln_g'], p['conv_ln_b'], LN_EPS)).astype(x.dtype)

    gates = jax.nn.sigmoid(proj[..., O_G:].astype(f32))
    g_a, g_b, g_c = gates[..., :D_MODEL], gates[..., D_MODEL:2 * D_MODEL], gates[..., 2 * D_MODEL:]
    m = g_a * (o_a @ p['w_pa']) + g_b * (o_b @ p['w_pb']) + g_c * (o_c @ p['w_pc'])
    x = x + (m.astype(x.dtype) @ p['w_o'])

    h2 = rms_norm(x, p['norm2_g'])
    x = x + jnp.square(jax.nn.relu(h2 @ p['w_mlp1'])) @ p['w_mlp2']
    return x, kh.astype(x.dtype), vh, s_new.astype(x.dtype), new_shift, new_conv


def setup_inputs(seed: int = 0) -> dict:
    key = jax.random.key(seed)
    ks = iter(jax.random.split(key, 48))
    f32 = jnp.float32
    n_pages = PAST_LEN // PAGE_SIZE
    n_used = DEC_BATCH * n_pages
    n_pool = n_used + max(1, n_used // 4)

    def nrm(shape, scale):
        return jax.random.normal(next(ks), shape, f32) * scale

    def gain(shape):
        return 1.0 + nrm(shape, 0.02)

    d = {}
    d['x_prompt'] = nrm((BATCH, SEQ, D_MODEL), 1.0)
    d['x_sample'] = nrm((DEC_BATCH, DEC_SEQ, D_MODEL), 1.0)
    d['cache_k'] = nrm((DEPTH, n_pool, PAGE_SIZE, N_HEADS_A, 2, HEAD_DIM), 1.0)
    d['cache_v'] = nrm((DEPTH, n_pool, PAGE_SIZE, N_HEADS_A, 2 * HEAD_DIM), 1.0)
    d['page_table'] = jax.random.permutation(next(ks), n_pool)[:n_used].reshape(DEC_BATCH, n_pages).astype(jnp.int32)
    d['state_rwkv'] = nrm((DEPTH, DEC_BATCH, H_R, RWKV_HEAD, RWKV_HEAD), 0.5)
    d['state_shift'] = nrm((DEPTH, DEC_BATCH, RWKV_IN), 1.0)
    d['state_conv'] = nrm((DEPTH, DEC_BATCH, CONV_W - 1, C_C), 0.5)
    d['norm1_g'] = gain((DEPTH, D_MODEL))
    d['w_in'] = nrm((DEPTH, D_MODEL, N_IN), D_MODEL ** -0.5)
    d['q_norm_g'] = gain((DEPTH, HEAD_DIM))
    d['k_norm_g'] = gain((DEPTH, HEAD_DIM))
    d['lambda_q1'] = nrm((DEPTH, HEAD_DIM), 0.1)
    d['lambda_k1'] = nrm((DEPTH, HEAD_DIM), 0.1)
    d['lambda_q2'] = nrm((DEPTH, HEAD_DIM), 0.1)
    d['lambda_k2'] = nrm((DEPTH, HEAD_DIM), 0.1)
    d['subln_g'] = gain((DEPTH, 2 * HEAD_DIM))
    d['rwkv_mu'] = jax.random.uniform(next(ks), (DEPTH, RWKV_IN), f32)
    d['rwkv_w0'] = nrm((DEPTH, C_R), 1.0)
    d['rwkv_w2'] = nrm((DEPTH, DECAY_RANK, C_R), DECAY_RANK ** -0.5)
    d['rwkv_a0'] = nrm((DEPTH, C_R), 0.5)
    d['rwkv_a2'] = nrm((DEPTH, A_RANK, C_R), A_RANK ** -0.5)
    d['rwkv_g2'] = nrm((DEPTH, G_RANK, C_R), G_RANK ** -0.5)
    d['rwkv_kk'] = 0.85 + nrm((DEPTH, C_R), 0.02)
    d['rwkv_ka'] = gain((DEPTH, C_R))
    d['rwkv_rk'] = nrm((DEPTH, H_R, RWKV_HEAD), 0.1)
    d['rwkv_lnx_g'] = gain((DEPTH, C_R))
    d['rwkv_lnx_b'] = nrm((DEPTH, C_R), 0.02)
    d['conv_w'] = nrm((DEPTH, CONV_W, C_C), CONV_W ** -0.5)
    d['conv_b'] = nrm((DEPTH, C_C), 0.02)
    d['conv_ln_g'] = gain((DEPTH, C_C))
    d['conv_ln_b'] = nrm((DEPTH, C_C), 0.02)
    d['w_pa'] = nrm((DEPTH, A_WIDTH, D_MODEL), A_WIDTH ** -0.5)
    d['w_pb'] = nrm((DEPTH, C_R, D_MODEL), C_R ** -0.5)
    d['w_pc'] = nrm((DEPTH, C_C, D_MODEL), C_C ** -0.5)
    d['w_o'] = nrm((DEPTH, D_MODEL, D_MODEL), D_MODEL ** -0.5)
    d['norm2_g'] = gain((DEPTH, D_MODEL))
    d['w_mlp1'] = nrm((DEPTH, D_MODEL, D_FF), D_MODEL ** -0.5)
    d['w_mlp2'] = nrm((DEPTH, D_FF, D_MODEL), D_FF ** -0.5)
    return d


def reference(x_prompt, x_sample, cache_k, cache_v, page_table, state_rwkv, state_shift, state_conv,
              norm1_g, w_in, q_norm_g, k_norm_g, lambda_q1, lambda_k1, lambda_q2, lambda_k2, subln_g,
              rwkv_mu, rwkv_w0, rwkv_w2, rwkv_a0, rwkv_a2, rwkv_g2, rwkv_kk, rwkv_ka, rwkv_rk,
              rwkv_lnx_g, rwkv_lnx_b, conv_w, conv_b, conv_ln_g, conv_ln_b,
              w_pa, w_pb, w_pc, w_o, norm2_g, w_mlp1, w_mlp2):
    b_p, t_p, _ = x_prompt.shape
    b_s, t_s, _ = x_sample.shape
    past_len = page_table.shape[1] * PAGE_SIZE
    pos_p = jnp.arange(t_p)
    pos_s = past_len + jnp.arange(t_s)
    yp, ys = x_prompt, x_sample
    kp_l, vp_l, ks_l, vs_l = [], [], [], []
    rp_l, rs_l, shp_l, shs_l, cp_l, cs_l = [], [], [], [], [], []
    for l in range(DEPTH):
        p = dict(norm1_g=norm1_g[l], w_in=w_in[l], q_norm_g=q_norm_g[l], k_norm_g=k_norm_g[l],
                 lambda_q1=lambda_q1[l], lambda_k1=lambda_k1[l], lambda_q2=lambda_q2[l], lambda_k2=lambda_k2[l],
                 subln_g=subln_g[l], rwkv_mu=rwkv_mu[l], rwkv_w0=rwkv_w0[l], rwkv_w2=rwkv_w2[l],
                 rwkv_a0=rwkv_a0[l], rwkv_a2=rwkv_a2[l], rwkv_g2=rwkv_g2[l], rwkv_kk=rwkv_kk[l],
                 rwkv_ka=rwkv_ka[l], rwkv_rk=rwkv_rk[l], rwkv_lnx_g=rwkv_lnx_g[l], rwkv_lnx_b=rwkv_lnx_b[l],
                 conv_w=conv_w[l], conv_b=conv_b[l], conv_ln_g=conv_ln_g[l], conv_ln_b=conv_ln_b[l],
                 w_pa=w_pa[l], w_pb=w_pb[l], w_pc=w_pc[l], w_o=w_o[l], norm2_g=norm2_g[l],
                 w_mlp1=w_mlp1[l], w_mlp2=w_mlp2[l])
        lambda_init = 0.8 - 0.6 * math.exp(-0.3 * l)
        yp, k_new, v_new, s_new, sh_new, c_new = layer(
            yp, pos_p, None, None,
            jnp.zeros((b_p, H_R, RWKV_HEAD, RWKV_HEAD), jnp.float32),
            jnp.zeros((b_p, RWKV_IN), x_prompt.dtype),
            jnp.zeros((b_p, CONV_W - 1, C_C), x_prompt.dtype), p, lambda_init)
        kp_l.append(k_new); vp_l.append(v_new); rp_l.append(s_new); shp_l.append(sh_new); cp_l.append(c_new)
        past_k = cache_k[l, page_table].reshape(b_s, past_len, N_HEADS_A, 2, HEAD_DIM)
        past_v = cache_v[l, page_table].reshape(b_s, past_len, N_HEADS_A, 2 * HEAD_DIM)
        ys, k_new, v_new, s_new, sh_new, c_new = layer(
            ys, pos_s, past_k, past_v, state_rwkv[l].astype(jnp.float32), state_shift[l], state_conv[l],
            p, lambda_init)
        ks_l.append(k_new); vs_l.append(v_new); rs_l.append(s_new); shs_l.append(sh_new); cs_l.append(c_new)
    return (yp, ys, jnp.stack(kp_l), jnp.stack(vp_l), jnp.stack(ks_l), jnp.stack(vs_l),
            jnp.stack(rp_l), jnp.stack(rs_l), jnp.stack(shp_l), jnp.stack(shs_l),
            jnp.stack(cp_l), jnp.stack(cs_l))
```

```python
import functools
import math

import jax
import jax.numpy as jnp
from jax import lax
from jax.experimental import pallas as pl
from jax.experimental.pallas import tpu as pltpu

F32 = jnp.float32
BF16 = jnp.bfloat16

N_HEADS_A = 8
HEAD_DIM = 128
A_WIDTH = N_HEADS_A * 2 * HEAD_DIM
ROPE_THETA = 10000.0
RWKV_HEAD = 64
DECAY_RANK = 64
A_RANK = 64
G_RANK = 32
CONV_W = 31
PAGE_SIZE = 128
RMS_EPS = 1e-6
LN_EPS = 1e-5
LN_X_EPS = 64e-5

LANE = 128
RWKV_CHUNK = 64
CONV_HALO = 32
NEG_BIG = -1e30


def _sigmoid(x):
    return 1.0 / (1.0 + jnp.exp(-x))


def _cparams(*sem):
    return pltpu.CompilerParams(dimension_semantics=sem, vmem_limit_bytes=56 * 1024 * 1024)


def _pick(n, cands):
    for c in cands:
        if n % c == 0:
            return c
    raise ValueError(f"no tile for {n}")


def _rmsnorm_kernel(x_ref, g_ref, o_ref):
    x = x_ref[...]
    ms = jnp.mean(x * x, axis=-1, keepdims=True)
    o_ref[...] = (x * lax.rsqrt(ms + RMS_EPS) * g_ref[...]).astype(o_ref.dtype)


def rmsnorm_bf16(x, g):
    m, d = x.shape
    tm = _pick(m, (512, 256, 128, 64, 32, 16))
    return pl.pallas_call(
        _rmsnorm_kernel,
        grid=(m // tm,),
        in_specs=[pl.BlockSpec((tm, d), lambda i: (i, 0)), pl.BlockSpec((1, d), lambda i: (0, 0))],
        out_specs=pl.BlockSpec((tm, d), lambda i: (i, 0)),
        out_shape=jax.ShapeDtypeStruct((m, d), BF16),
        compiler_params=_cparams("parallel"),
        name="rmsnorm",
    )(x, g.reshape(1, d))


def _matmul_kernel(*refs, nk, epilogue):
    if epilogue == "residual":
        x_ref, w_ref, res_ref, o_ref = refs[:4]
        rest = refs[4:]
    else:
        x_ref, w_ref, o_ref = refs[:3]
        res_ref = None
        rest = refs[3:]

    def finish(acc):
        if epilogue == "relu2":
            r = jnp.maximum(acc, 0.0)
            acc = r * r
        elif epilogue == "residual":
            acc = acc + res_ref[...]
        o_ref[...] = acc.astype(o_ref.dtype)

    part = jnp.dot(x_ref[...], w_ref[...], preferred_element_type=F32)
    if nk == 1:
        finish(part)
    else:
        acc_ref = rest[0]
        k = pl.program_id(2)

        @pl.when(k == 0)
        def _():
            acc_ref[...] = part

        @pl.when(k > 0)
        def _():
            acc_ref[...] += part

        @pl.when(k == nk - 1)
        def _():
            finish(acc_ref[...])


def matmul(x, w, out_dtype, epilogue=None, res=None):
    m, kdim = x.shape
    n = w.shape[1]
    tm = _pick(m, (1024, 512, 256, 128, 64, 32, 16))
    tn = _pick(n, (512, 256, 128))
    tk = _pick(kdim, (2048, 1024, 512))
    nk = kdim // tk
    in_specs = [pl.BlockSpec((tm, tk), lambda i, j, k: (i, k)),
                pl.BlockSpec((tk, tn), lambda i, j, k: (k, j))]
    args = [x, w]
    if epilogue == "residual":
        in_specs.append(pl.BlockSpec((tm, tn), lambda i, j, k: (i, j)))
        args.append(res)
    scratch = [pltpu.VMEM((tm, tn), F32)] if nk > 1 else []
    return pl.pallas_call(
        functools.partial(_matmul_kernel, nk=nk, epilogue=epilogue),
        grid=(m // tm, n // tn, nk),
        in_specs=in_specs,
        out_specs=pl.BlockSpec((tm, tn), lambda i, j, k: (i, j)),
        out_shape=jax.ShapeDtypeStruct((m, n), out_dtype),
        scratch_shapes=scratch,
        compiler_params=_cparams("parallel", "parallel", "arbitrary"),
        name="matmul_" + (epilogue or "plain"),
    )(*args)


def _qknorm_rope_kernel(x_ref, g_ref, cos_ref, sin_ref, o_ref):
    g = g_ref[...]
    cosf = cos_ref[...]
    sins = sin_ref[...]
    for grp in range(x_ref.shape[1] // HEAD_DIM):
        sl = slice(grp * HEAD_DIM, (grp + 1) * HEAD_DIM)
        x = x_ref[:, sl]
        ms = jnp.mean(x * x, axis=-1, keepdims=True)
        y = x * lax.rsqrt(ms + RMS_EPS) * g
        o_ref[:, sl] = y * cosf + pltpu.roll(y, HEAD_DIM // 2, 1) * sins


def qknorm_rope(proj, gains, cosf, sins):
    m = proj.shape[0]
    p = cosf.shape[0]
    tm = _pick(math.gcd(m, p), (256, 128, 8))
    npb = p // tm
    return pl.pallas_call(
        _qknorm_rope_kernel,
        grid=(m // tm, 2),
        in_specs=[pl.BlockSpec((tm, A_WIDTH), lambda i, s: (i, s)),
                  pl.BlockSpec((None, 1, HEAD_DIM), lambda i, s: (s, 0, 0)),
                  pl.BlockSpec((tm, HEAD_DIM), lambda i, s: (i % npb, 0)),
                  pl.BlockSpec((tm, HEAD_DIM), lambda i, s: (i % npb, 0))],
        out_specs=pl.BlockSpec((None, tm, A_WIDTH), lambda i, s: (s, i, 0)),
        out_shape=jax.ShapeDtypeStruct((2, m, A_WIDTH), F32),
        compiler_params=_cparams("parallel", "parallel"),
        name="qknorm_rope",
    )(proj, gains, cosf, sins)


def _subln(o, g, scale):
    ms = jnp.mean(o * o, axis=-1, keepdims=True)
    return o * lax.rsqrt(ms + LN_EPS) * g * scale


def _flash_kernel(lam_ref, q_ref, k_ref, v_ref, g_ref, o_ref, m_ref, l_ref, acc_ref, *, out_scale):
    qi = pl.program_id(2)
    ki = pl.program_id(3)
    tq = q_ref.shape[0]
    tk = k_ref.shape[0]

    @pl.when(ki == 0)
    def _():
        m_ref[...] = jnp.full(m_ref.shape, NEG_BIG, F32)
        l_ref[...] = jnp.zeros(l_ref.shape, F32)
        acc_ref[...] = jnp.zeros(acc_ref.shape, F32)

    def update(masked):
        vb = v_ref[...].astype(BF16)
        for mp in range(2):
            sl = slice(mp * HEAD_DIM, (mp + 1) * HEAD_DIM)
            q = (q_ref[:, sl] * (HEAD_DIM ** -0.5)).astype(BF16)
            k = k_ref[:, sl].astype(BF16)
            s = lax.dot_general(q, k, (((1,), (1,)), ((), ())), preferred_element_type=F32)
            if masked:
                row = lax.broadcasted_iota(jnp.int32, (tq, tk), 0)
                col = lax.broadcasted_iota(jnp.int32, (tq, tk), 1)
                s = jnp.where(col <= row, s, NEG_BIG)
            m_prev = m_ref[mp]
            m_new = jnp.maximum(m_prev, jnp.max(s, axis=-1, keepdims=True))
            p = jnp.exp(s - m_new)
            alpha = jnp.exp(m_prev - m_new)
            l_ref[mp] = alpha * l_ref[mp] + jnp.sum(p, axis=-1, keepdims=True)
            acc_ref[mp] = alpha * acc_ref[mp] + jnp.dot(p.astype(BF16), vb, preferred_element_type=F32)
            m_ref[mp] = m_new

    @pl.when(ki < qi)
    def _():
        update(False)

    @pl.when(ki == qi)
    def _():
        update(True)
        lam = lam_ref[0]
        o = acc_ref[0] / l_ref[0] - lam * (acc_ref[1] / l_ref[1])
        o_ref[...] = _subln(o, g_ref[...], out_scale).astype(o_ref.dtype)


def flash_diff_attention(lam, qk, proj, v_col0, subln_g, batch, seq, out_scale):
    m = batch * seq
    tq = _pick(seq, (512, 256, 128))
    nq = seq // tq
    hw = 2 * HEAD_DIM
    vb0 = v_col0 // hw
    return pl.pallas_call(
        functools.partial(_flash_kernel, out_scale=out_scale),
        grid=(batch, N_HEADS_A, nq, nq),
        in_specs=[pl.BlockSpec(memory_space=pltpu.SMEM),
                  pl.BlockSpec((None, tq, hw), lambda b, h, qi, ki: (0, b * nq + qi, h)),
                  pl.BlockSpec((None, tq, hw), lambda b, h, qi, ki: (1, b * nq + jnp.minimum(ki, qi), h)),
                  pl.BlockSpec((tq, hw), lambda b, h, qi, ki: (b * nq + jnp.minimum(ki, qi), vb0 + h)),
                  pl.BlockSpec((1, hw), lambda b, h, qi, ki: (0, 0))],
        out_specs=pl.BlockSpec((tq, hw), lambda b, h, qi, ki: (b * nq + qi, h)),
        out_shape=jax.ShapeDtypeStruct((m, A_WIDTH), BF16),
        scratch_shapes=[pltpu.VMEM((2, tq, 1), F32), pltpu.VMEM((2, tq, 1), F32),
                        pltpu.VMEM((2, tq, hw), F32)],
        compiler_params=_cparams("parallel", "parallel", "parallel", "arbitrary"),
        name="flash_diff_attention",
    )(lam, qk, qk, proj, subln_g.reshape(1, hw))


def _decode_kernel(pt_ref, lam_ref, q_ref, kn_ref, vn_ref, kc_ref, vc_ref, g_ref, o_ref,
                   qrow_ref, m_ref, l_ref, acc_ref, *, n_pages, out_scale):
    p = pl.program_id(1)
    ng = 2 * N_HEADS_A
    hw = 2 * HEAD_DIM

    @pl.when(p == 0)
    def _():
        row = lax.broadcasted_iota(jnp.int32, (ng, A_WIDTH), 0)
        col = lax.broadcasted_iota(jnp.int32, (ng, A_WIDTH), 1)
        qb = jnp.broadcast_to(q_ref[...] * (HEAD_DIM ** -0.5), (ng, A_WIDTH))
        qrow_ref[...] = jnp.where(col // HEAD_DIM == row, qb, 0.0)
        m_ref[...] = jnp.full(m_ref.shape, NEG_BIG, F32)
        l_ref[...] = jnp.zeros(l_ref.shape, F32)
        acc_ref[...] = jnp.zeros(acc_ref.shape, F32)

    s = lax.dot_general(qrow_ref[...].astype(BF16), kc_ref[...].astype(BF16),
                        (((1,), (1,)), ((), ())), preferred_element_type=F32)
    m_prev = m_ref[...]
    m_new = jnp.maximum(m_prev, jnp.max(s, axis=-1, keepdims=True))
    pr = jnp.exp(s - m_new)
    alpha = jnp.exp(m_prev - m_new)
    l_ref[...] = alpha * l_ref[...] + jnp.sum(pr, axis=-1, keepdims=True)
    acc_ref[...] = alpha * acc_ref[...] + jnp.dot(pr.astype(BF16), vc_ref[...].astype(BF16),
                                                  preferred_element_type=F32)
    m_ref[...] = m_new

    @pl.when(p == n_pages - 1)
    def _():
        s_new = jnp.sum(qrow_ref[...] * kn_ref[...], axis=-1, keepdims=True)
        m_prev = m_ref[...]
        m_fin = jnp.maximum(m_prev, s_new)
        p_new = jnp.exp(s_new - m_fin)
        alpha = jnp.exp(m_prev - m_fin)
        l_fin = alpha * l_ref[...] + p_new
        acc = alpha * acc_ref[...] + p_new * vn_ref[...]
        d = acc / l_fin
        row = lax.broadcasted_iota(jnp.int32, (ng, A_WIDTH), 0)
        col = lax.broadcasted_iota(jnp.int32, (ng, A_WIDTH), 1)
        coef = jnp.where(row % 2 == 0, 1.0, -lam_ref[0])
        d = jnp.where(col // hw == row // 2, d * coef, 0.0)
        o = jnp.sum(d, axis=0, keepdims=True)
        g = g_ref[...]
        for h in range(N_HEADS_A):
            sl = slice(h * hw, (h + 1) * hw)
            o_ref[:, sl] = _subln(o[:, sl], g, out_scale).astype(o_ref.dtype)


def paged_diff_attention(lam, qk, proj, v_col0, cache_k, cache_v, layer_idx, page_table, subln_g, out_scale):
    bs, n_pages = page_table.shape
    ng = 2 * N_HEADS_A
    hw = 2 * HEAD_DIM
    qk4 = qk.reshape(2, bs, 1, A_WIDTH)
    proj3 = proj.reshape(bs, 1, proj.shape[1])
    vb0 = v_col0 // A_WIDTH
    grid_spec = pltpu.PrefetchScalarGridSpec(
        num_scalar_prefetch=1,
        grid=(bs, n_pages),
        in_specs=[pl.BlockSpec(memory_space=pltpu.SMEM),
                  pl.BlockSpec((None, None, 1, A_WIDTH), lambda b, p, pt: (0, b, 0, 0)),
                  pl.BlockSpec((None, None, 1, A_WIDTH), lambda b, p, pt: (1, b, 0, 0)),
                  pl.BlockSpec((None, 1, A_WIDTH), lambda b, p, pt: (b, 0, vb0)),
                  pl.BlockSpec((None, None, PAGE_SIZE, A_WIDTH),
                               lambda b, p, pt: (layer_idx, pt[b * n_pages + p], 0, 0)),
                  pl.BlockSpec((None, None, PAGE_SIZE, A_WIDTH),
                               lambda b, p, pt: (layer_idx, pt[b * n_pages + p], 0, 0)),
                  pl.BlockSpec((1, hw), lambda b, p, pt: (0, 0))],
        out_specs=pl.BlockSpec((None, 1, A_WIDTH), lambda b, p, pt: (b, 0, 0)),
        scratch_shapes=[pltpu.VMEM((ng, A_WIDTH), F32), pltpu.VMEM((ng, 1), F32),
                        pltpu.VMEM((ng, 1), F32), pltpu.VMEM((ng, A_WIDTH), F32)],
    )
    out = pl.pallas_call(
        functools.partial(_decode_kernel, n_pages=n_pages, out_scale=out_scale),
        grid_spec=grid_spec,
        out_shape=jax.ShapeDtypeStruct((bs, 1, A_WIDTH), BF16),
        compiler_params=_cparams("parallel", "arbitrary"),
        name="paged_diff_attention",
    )(page_table.reshape(-1), lam, qk4, qk4, proj3, cache_k, cache_v, subln_g.reshape(1, hw))
    return out.reshape(bs, A_WIDTH)


def _rwkv_pad_cols(x, c_r):
    lead = x.shape[:-1]
    z = lambda n: jnp.zeros(lead + (n,), x.dtype)
    o = 3 * c_r
    return jnp.concatenate([x[..., :o],
                            x[..., o:o + DECAY_RANK], z(LANE - DECAY_RANK),
                            x[..., o + DECAY_RANK:o + DECAY_RANK + A_RANK], z(LANE - A_RANK),
                            x[..., o + DECAY_RANK + A_RANK:], z(LANE - G_RANK),
                            z(LANE)], axis=-1)


def _rwkv_unpad_cols(x, c_r):
    o = 3 * c_r
    return jnp.concatenate([x[..., :o], x[..., o:o + DECAY_RANK],
                            x[..., o + LANE:o + LANE + A_RANK],
                            x[..., o + 2 * LANE:o + 2 * LANE + G_RANK]], axis=-1)


def _rwkv_premix_kernel(pr_ref, prev_ref, mu_ref, w0_ref, w2_ref, a0_ref, a2_ref, g2_ref, kkp_ref, kap_ref,
                        o_ref, *, c_r, seq, single_step):
    i = pl.program_id(0)
    tm = pr_ref.shape[0]
    pr = pr_ref[...]
    if single_step:
        prev = prev_ref[...]
    else:
        rolled = pltpu.roll(pr, 1, 0)
        first = jnp.where((i * tm) % seq == 0, 0.0, 1.0) * prev_ref[7:8, :]
        row = lax.broadcasted_iota(jnp.int32, pr.shape, 0)
        prev = jnp.where(row == 0, first, rolled)
    ps = pr + mu_ref[...] * (prev - pr)
    r = ps[:, 0:c_r]
    kr = ps[:, c_r:2 * c_r]
    vr = ps[:, 2 * c_r:3 * c_r]
    wd = ps[:, 3 * c_r:3 * c_r + LANE]
    ad = ps[:, 3 * c_r + LANE:3 * c_r + 2 * LANE]
    gd = ps[:, 3 * c_r + 2 * LANE:3 * c_r + 3 * LANE]
    z = -(w0_ref[...] + jnp.dot(jnp.tanh(wd).astype(BF16), w2_ref[...], preferred_element_type=F32))
    softplus = jnp.maximum(z, 0.0) + jnp.log(1.0 + jnp.exp(-jnp.abs(z)))
    w_log = -softplus - 0.5
    a = _sigmoid(a0_ref[...] + jnp.dot(ad.astype(BF16), a2_ref[...], preferred_element_type=F32))
    g = jnp.dot(_sigmoid(gd).astype(BF16), g2_ref[...], preferred_element_type=F32)
    o_ref[0] = r
    o_ref[1] = kr * (1.0 + (a - 1.0) * kap_ref[...])
    o_ref[2] = vr
    o_ref[3] = -jnp.exp(w_log)
    o_ref[4] = kr * kkp_ref[...]
    o_ref[5] = a
    o_ref[6] = g


def rwkv_premix(proj, r_col0, prev, p, seq, single_step):
    m = proj.shape[0]
    c_r = p["w0"].shape[1]
    wr = 3 * c_r + 4 * LANE
    cb = r_col0 // wr
    tm = _pick(math.gcd(m, seq) if not single_step else m, (256, 128, 64, 32, 16, 8))
    if single_step:
        prev_spec = pl.BlockSpec((tm, wr), lambda i: (i, 0))
        prev_arg = prev
    else:
        prev_spec = pl.BlockSpec((8, wr), lambda i: (jnp.maximum(i * (tm // 8) - 1, 0), cb))
        prev_arg = proj
    vec = lambda n: pl.BlockSpec((1, n), lambda i: (0, 0))
    mat = lambda: pl.BlockSpec((LANE, c_r), lambda i: (0, 0))
    return pl.pallas_call(
        functools.partial(_rwkv_premix_kernel, c_r=c_r, seq=seq, single_step=single_step),
        grid=(m // tm,),
        in_specs=[pl.BlockSpec((tm, wr), lambda i: (i, cb)), prev_spec, vec(wr),
                  vec(c_r), mat(), vec(c_r), mat(), mat(), vec(c_r), vec(c_r)],
        out_specs=pl.BlockSpec((7, tm, c_r), lambda i: (0, i, 0)),
        out_shape=jax.ShapeDtypeStruct((7, m, c_r), F32),
        compiler_params=_cparams("parallel"),
        name="rwkv_premix",
    )(proj, prev_arg, p["mu"], p["w0"], p["w2"], p["a0"], p["a2"], p["g2"], p["kk"], p["ka"])


def _split3(x):
    hi = x.astype(BF16)
    r1 = x - hi.astype(F32)
    mid = r1.astype(BF16)
    lo = (r1 - mid.astype(F32)).astype(BF16)
    return hi, mid, lo


def _halfsum(x, m0):
    s0 = jnp.sum(jnp.where(m0, x, 0.0), axis=-1, keepdims=True)
    s1 = jnp.sum(jnp.where(m0, 0.0, x), axis=-1, keepdims=True)
    return jnp.where(m0, s0, s1)


def _bdot(a, b):
    return jnp.dot(a.astype(BF16), b.astype(BF16), preferred_element_type=F32)


def _bdot_nt(a, b):
    return lax.dot_general(a.astype(BF16), b.astype(BF16), (((1,), (1,)), ((), ())),
                           preferred_element_type=F32)


def _rwkv_scan_kernel(x_ref, rk_ref, lng_ref, lnb_ref, o_ref, s_out_ref, s_ref):
    c = pl.program_id(2)
    C = x_ref.shape[1]
    N = RWKV_HEAD

    @pl.when(c == 0)
    def _():
        s_ref[...] = jnp.zeros(s_ref.shape, F32)

    r = x_ref[0]
    k = x_ref[1]
    v = x_ref[2]
    lw = x_ref[3]
    kk = x_ref[4]
    a = x_ref[5]
    g = x_ref[6]
    lane = lax.broadcasted_iota(jnp.int32, (C, LANE), 1)
    m0 = lane < N
    kk = kk * lax.rsqrt(_halfsum(kk * kk, m0) + 1e-12)

    trow = lax.broadcasted_iota(jnp.int32, (C, C), 0)
    tcol = lax.broadcasted_iota(jnp.int32, (C, C), 1)
    tri = jnp.where(tcol <= trow, 1.0, 0.0).astype(BF16)
    hi, mid, lo = _split3(lw)
    cs = (jnp.dot(tri, hi, preferred_element_type=F32) + jnp.dot(tri, mid, preferred_element_type=F32)
          + jnp.dot(tri, lo, preferred_element_type=F32))
    cs_end = cs[C - 1:C, :]
    e_neg = jnp.exp(-cs)
    al = -kk * jnp.exp(cs - lw)
    rt = r * jnp.exp(cs)
    kb = kk * a
    bh = kb * e_neg
    kh = k * e_neg
    e_end = jnp.exp(cs_end - cs)
    b_end = kb * e_end
    k_end = k * e_end

    S = s_ref[...]
    lhs = jnp.concatenate([jnp.where(m0, al, 0.0), jnp.where(m0, rt, 0.0),
                           jnp.where(m0, 0.0, al), jnp.where(m0, 0.0, rt)], axis=0)
    rhs = jnp.concatenate([bh, kh], axis=0)
    A = _bdot_nt(lhs, rhs)
    strict = tcol < trow
    incl = tcol <= trow
    from_state = _bdot_nt(jnp.concatenate([al, rt], axis=0), S)

    u_heads = []
    y_acc = from_state[C:2 * C]
    for h in range(2):
        base = 2 * C * h
        a_ab = jnp.where(strict, A[base:base + C, 0:C], 0.0)
        a_ak = jnp.where(strict, A[base:base + C, C:2 * C], 0.0)
        a_rb = jnp.where(incl, A[base + C:base + 2 * C, 0:C], 0.0)
        a_rk = jnp.where(incl, A[base + C:base + 2 * C, C:2 * C], 0.0)
        u = from_state[0:C] + _bdot(a_ak, v)
        npow = a_ab
        n = 1
        while n < C:
            u = u + _bdot(npow, u)
            n *= 2
            if n < C:
                npow = _bdot(npow, npow)
        u_heads.append(u)
        y_h = _bdot(jnp.concatenate([a_rb, a_rk], axis=1), jnp.concatenate([u, v], axis=0))
        y_acc = y_acc + jnp.where(m0 if h == 0 else jnp.logical_not(m0), y_h, 0.0)
    u = jnp.where(m0, u_heads[0], u_heads[1])
    y = y_acc

    uv_t = jnp.concatenate([u, v], axis=0).T
    upd = _bdot(uv_t, jnp.concatenate([b_end, k_end], axis=0))
    vrow = lax.broadcasted_iota(jnp.int32, (LANE, LANE), 0)
    kcol = lax.broadcasted_iota(jnp.int32, (LANE, LANE), 1)
    same_head = (vrow < N) == (kcol < N)
    s_new = jnp.where(same_head, S * jnp.exp(cs_end) + upd, 0.0)
    s_ref[...] = s_new
    s_out_ref[...] = s_new

    mu = _halfsum(y, m0) * (1.0 / N)
    d = y - mu
    var = _halfsum(d * d, m0) * (1.0 / N)
    yn = d * lax.rsqrt(var + LN_X_EPS) * lng_ref[...] + lnb_ref[...]
    bonus = _halfsum(r * k * rk_ref[...], m0) * v
    o_ref[...] = ((yn + bonus) * g).astype(o_ref.dtype)


def rwkv_scan(x7, rk, lnx_g, lnx_b, batch, seq):
    m, c_r = x7.shape[1], x7.shape[2]
    C = RWKV_CHUNK
    nc = seq // C
    npair = c_r // LANE
    vec = lambda: pl.BlockSpec((1, LANE), lambda b, pr, c: (0, pr))
    return pl.pallas_call(
        _rwkv_scan_kernel,
        grid=(batch, npair, nc),
        in_specs=[pl.BlockSpec((7, C, LANE), lambda b, pr, c: (0, b * nc + c, pr)), vec(), vec(), vec()],
        out_specs=[pl.BlockSpec((C, LANE), lambda b, pr, c: (b * nc + c, pr)),
                   pl.BlockSpec((None, None, LANE, LANE), lambda b, pr, c: (b, pr, 0, 0))],
        out_shape=[jax.ShapeDtypeStruct((m, c_r), BF16),
                   jax.ShapeDtypeStruct((batch, npair, LANE, LANE), F32)],
        scratch_shapes=[pltpu.VMEM((LANE, LANE), F32)],
        compiler_params=_cparams("parallel", "parallel", "arbitrary"),
        name="rwkv_scan",
    )(x7, rk, lnx_g, lnx_b)


def _rwkv_step_kernel(s_ref, x_ref, rk_ref, lng_ref, lnb_ref, s_out_ref, o_ref):
    N = RWKV_HEAD
    S = s_ref[...]
    r = x_ref[0]
    k = x_ref[1]
    v = x_ref[2]
    w = jnp.exp(x_ref[3])
    kk = x_ref[4]
    a = x_ref[5]
    g = x_ref[6]
    kk = kk * lax.rsqrt(jnp.sum(kk * kk, axis=-1, keepdims=True) + 1e-12)
    eye = (lax.broadcasted_iota(jnp.int32, (1, N, N), 1) == lax.broadcasted_iota(jnp.int32, (1, N, N), 2))
    to_col = lambda row: jnp.sum(jnp.where(eye, row, 0.0), axis=-1, keepdims=True)
    to_row = lambda col: jnp.sum(jnp.where(eye, col, 0.0), axis=-2, keepdims=True)
    sk = -jnp.sum(S * kk, axis=-1, keepdims=True)
    s_new = S * w + sk * (kk * a) + to_col(v) * k
    s_out_ref[...] = s_new
    y = to_row(jnp.sum(s_new * r, axis=-1, keepdims=True))
    mu = jnp.mean(y, axis=-1, keepdims=True)
    d = y - mu
    var = jnp.mean(d * d, axis=-1, keepdims=True)
    yn = d * lax.rsqrt(var + LN_X_EPS) * lng_ref[...] + lnb_ref[...]
    bonus = jnp.sum(r * k * rk_ref[...], axis=-1, keepdims=True) * v
    o_ref[...] = (yn + bonus) * g


def rwkv_step(state, x7, rk, lnx_g, lnx_b):
    bsz, nh, N, _ = state.shape
    bb = _pick(bsz, (16, 8, 4, 2, 1))
    x5 = x7.reshape(7, bsz, nh, 1, N)
    pv = lambda t: t.reshape(nh, 1, N)
    vec = lambda: pl.BlockSpec((None, 1, N), lambda i, h: (h, 0, 0))
    s_new, o = pl.pallas_call(
        _rwkv_step_kernel,
        grid=(bsz // bb, nh),
        in_specs=[pl.BlockSpec((bb, None, N, N), lambda i, h: (i, h, 0, 0)),
                  pl.BlockSpec((7, bb, None, 1, N), lambda i, h: (0, i, h, 0, 0)),
                  vec(), vec(), vec()],
        out_specs=[pl.BlockSpec((bb, None, N, N), lambda i, h: (i, h, 0, 0)),
                   pl.BlockSpec((bb, None, 1, N), lambda i, h: (i, h, 0, 0))],
        out_shape=[jax.ShapeDtypeStruct(state.shape, F32),
                   jax.ShapeDtypeStruct((bsz, nh, 1, N), F32)],
        compiler_params=_cparams("parallel", "parallel"),
        name="rwkv_step",
    )(state, x5, pv(rk), pv(lnx_g), pv(lnx_b))
    return s_new, o.reshape(bsz, nh * N)


def _ln_silu(y, g, b):
    mu = jnp.mean(y, axis=-1, keepdims=True)
    d = y - mu
    var = jnp.mean(d * d, axis=-1, keepdims=True)
    z = d * lax.rsqrt(var + LN_EPS) * g + b
    return z * _sigmoid(z)


def _conv_seq_kernel(cu_ref, halo_ref, w_ref, b_ref, lg_ref, lb_ref, o_ref, tail_ref, full_ref, *, seq):
    i = pl.program_id(0)
    tm = cu_ref.shape[0]
    c = cu_ref.shape[1] // 2
    glu = lambda ref: ref[:, 0:c] * _sigmoid(ref[:, c:2 * c])
    keep = jnp.where((i * tm) % seq == 0, 0.0, 1.0)
    full_ref[0:CONV_HALO, :] = glu(halo_ref) * keep
    full_ref[CONV_HALO:CONV_HALO + tm, :] = glu(cu_ref)
    acc = jnp.zeros((tm, c), F32) + b_ref[...]
    off = CONV_HALO - (CONV_W - 1)
    for j in range(CONV_W):
        acc = acc + w_ref[j:j + 1, :] * full_ref[off + j:off + j + tm, :]
    o_ref[...] = _ln_silu(acc, lg_ref[...], lb_ref[...]).astype(o_ref.dtype)
    tail_ref[...] = full_ref[tm:tm + CONV_HALO, :]


def conv_seq(proj, c_col0, conv_w, conv_b, ln_g, ln_b, batch, seq):
    m = proj.shape[0]
    c = conv_w.shape[1]
    tm = _pick(seq, (256, 128, 64, 32))
    cb = c_col0 // (2 * c)
    hb = tm // CONV_HALO
    vec = lambda: pl.BlockSpec((1, c), lambda i: (0, 0))
    return pl.pallas_call(
        functools.partial(_conv_seq_kernel, seq=seq),
        grid=(m // tm,),
        in_specs=[pl.BlockSpec((tm, 2 * c), lambda i: (i, cb)),
                  pl.BlockSpec((CONV_HALO, 2 * c), lambda i: (jnp.maximum(i * hb - 1, 0), cb)),
                  pl.BlockSpec((CONV_W, c), lambda i: (0, 0)), vec(), vec(), vec()],
        out_specs=[pl.BlockSpec((tm, c), lambda i: (i, 0)),
                   pl.BlockSpec((None, CONV_HALO, c), lambda i: ((i * tm) // seq, 0, 0))],
        out_shape=[jax.ShapeDtypeStruct((m, c), BF16),
                   jax.ShapeDtypeStruct((batch, CONV_HALO, c), F32)],
        scratch_shapes=[pltpu.VMEM((CONV_HALO + tm, c), F32)],
        compiler_params=_cparams("arbitrary"),
        name="conv_seq",
    )(proj, proj, conv_w, conv_b.reshape(1, c), ln_g.reshape(1, c), ln_b.reshape(1, c))


def _conv_step_kernel(cu_ref, cp_ref, w_ref, b_ref, lg_ref, lb_ref, o_ref, cn_ref):
    c = cu_ref.shape[1] // 2
    u = cu_ref[:, 0:c] * _sigmoid(cu_ref[:, c:2 * c])
    acc = b_ref[...] + w_ref[CONV_W - 1:CONV_W, :] * u
    for j in range(CONV_W - 1):
        acc = acc + w_ref[j:j + 1, :] * cp_ref[:, j, :]
    o_ref[...] = _ln_silu(acc, lg_ref[...], lb_ref[...]).astype(o_ref.dtype)
    cn_ref[:, 0:CONV_W - 2, :] = cp_ref[:, 1:CONV_W - 1, :]
    cn_ref[:, CONV_W - 2, :] = u


def conv_step(proj, c_col0, conv_prev, conv_w, conv_b, ln_g, ln_b):
    bsz = proj.shape[0]
    c = conv_w.shape[1]
    bb = _pick(bsz, (8,))
    cb = c_col0 // (2 * c)
    vec = lambda: pl.BlockSpec((1, c), lambda i: (0, 0))
    return pl.pallas_call(
        _conv_step_kernel,
        grid=(bsz // bb,),
        in_specs=[pl.BlockSpec((bb, 2 * c), lambda i: (i, cb)),
                  pl.BlockSpec((bb, CONV_W - 1, c), lambda i: (i, 0, 0)),
                  pl.BlockSpec((CONV_W, c), lambda i: (0, 0)), vec(), vec(), vec()],
        out_specs=[pl.BlockSpec((bb, c), lambda i: (i, 0)),
                   pl.BlockSpec((bb, CONV_W - 1, c), lambda i: (i, 0, 0))],
        out_shape=[jax.ShapeDtypeStruct((bsz, c), BF16),
                   jax.ShapeDtypeStruct((bsz, CONV_W - 1, c), F32)],
        compiler_params=_cparams("parallel"),
        name="conv_step",
    )(proj, conv_prev, conv_w, conv_b.reshape(1, c), ln_g.reshape(1, c), ln_b.reshape(1, c))


def _merge_kernel(xa_ref, xb_ref, xc_ref, wa_ref, wb_ref, wc_ref, ga_ref, gb_ref, gc_ref, o_ref):
    m = _sigmoid(ga_ref[...]) * jnp.dot(xa_ref[...], wa_ref[...], preferred_element_type=F32)
    m = m + _sigmoid(gb_ref[...]) * jnp.dot(xb_ref[...], wb_ref[...], preferred_element_type=F32)
    m = m + _sigmoid(gc_ref[...]) * jnp.dot(xc_ref[...], wc_ref[...], preferred_element_type=F32)
    o_ref[...] = m.astype(o_ref.dtype)


def gated_merge(o_a, o_b, o_c, w_pa, w_pb, w_pc, proj, g_col0):
    m = o_a.shape[0]
    d = w_pa.shape[1]
    tm = _pick(m, (512, 256, 128, 64, 32, 16))
    tn = _pick(math.gcd(d, g_col0), (512, 256, 128))
    gb0 = g_col0 // tn
    nb = d // tn
    xs = lambda kd: pl.BlockSpec((tm, kd), lambda i, j: (i, 0))
    ws = lambda kd: pl.BlockSpec((kd, tn), lambda i, j: (0, j))
    gs = lambda br: pl.BlockSpec((tm, tn), lambda i, j: (i, gb0 + br * nb + j))
    return pl.pallas_call(
        _merge_kernel,
        grid=(m // tm, nb),
        in_specs=[xs(o_a.shape[1]), xs(o_b.shape[1]), xs(o_c.shape[1]),
                  ws(w_pa.shape[0]), ws(w_pb.shape[0]), ws(w_pc.shape[0]), gs(0), gs(1), gs(2)],
        out_specs=pl.BlockSpec((tm, tn), lambda i, j: (i, j)),
        out_shape=jax.ShapeDtypeStruct((m, d), BF16),
        compiler_params=_cparams("parallel", "parallel"),
        name="gated_merge",
    )(o_a, o_b, o_c, w_pa, w_pb, w_pc, proj, proj, proj)


def _rope_tables(pos):
    half = HEAD_DIM // 2
    freqs = ROPE_THETA ** (-jnp.arange(half, dtype=F32) * (2.0 / HEAD_DIM))
    ang = pos.astype(F32)[:, None] * freqs[None, :]
    cos, sin = jnp.cos(ang), jnp.sin(ang)
    return jnp.concatenate([cos, cos], axis=-1), jnp.concatenate([-sin, sin], axis=-1)


def _layer(x2d, batch, seq, lp, lam, lambda_init, tables, sample_state):
    d_model = x2d.shape[1]
    c_r = lp["w_pb"].shape[0]
    c_c = lp["w_pc"].shape[0]
    col_v = 2 * A_WIDTH
    col_g = 3 * A_WIDTH
    col_c = col_g + 3 * d_model
    col_r = col_c + 2 * c_c
    out_scale = 1.0 - lambda_init

    h = rmsnorm_bf16(x2d, lp["norm1_g"])
    proj = matmul(h, lp["w_in_cat"], F32)

    qk = qknorm_rope(proj, lp["qk_gain"], *tables)
    if sample_state is None:
        o_a = flash_diff_attention(lam, qk, proj, col_v, lp["subln_g"], batch, seq, out_scale)
    else:
        o_a = paged_diff_attention(lam, qk, proj, col_v, sample_state["cache_k"], sample_state["cache_v"],
                                   sample_state["layer"], sample_state["page_table"], lp["subln_g"], out_scale)
    new_k = qk[1]
    new_v = proj[:, col_v:col_v + A_WIDTH]

    pr_pad = proj[:, col_r:]
    if sample_state is None:
        x7 = rwkv_premix(proj, col_r, None, lp["rwkv"], seq, False)
        o_b, pair_state = rwkv_scan(x7, lp["rk"], lp["lnx_g"], lp["lnx_b"], batch, seq)
        nh = c_r // RWKV_HEAD
        ps = pair_state.reshape(batch, nh // 2, 2, RWKV_HEAD, 2, RWKV_HEAD)
        new_state = jnp.stack([ps[:, :, 0, :, 0, :], ps[:, :, 1, :, 1, :]], axis=2).reshape(
            batch, nh, RWKV_HEAD, RWKV_HEAD)
        new_shift = _rwkv_unpad_cols(pr_pad.reshape(batch, seq, -1)[:, -1], c_r)
    else:
        x7 = rwkv_premix(proj, col_r, _rwkv_pad_cols(sample_state["shift"], c_r), lp["rwkv"], seq, True)
        new_state, o_b = rwkv_step(sample_state["rwkv"], x7, lp["rk"], lp["lnx_g"], lp["lnx_b"])
        o_b = o_b.astype(BF16)
        new_shift = _rwkv_unpad_cols(pr_pad, c_r)

    if sample_state is None:
        o_c, tail = conv_seq(proj, col_c, lp["conv_w"], lp["conv_b"], lp["conv_ln_g"], lp["conv_ln_b"], batch, seq)
        new_conv = tail[:, CONV_HALO - (CONV_W - 1):]
    else:
        o_c, new_conv = conv_step(proj, col_c, sample_state["conv"], lp["conv_w"], lp["conv_b"],
                                  lp["conv_ln_g"], lp["conv_ln_b"])

    merged = gated_merge(o_a, o_b, o_c, lp["w_pa"], lp["w_pb"], lp["w_pc"], proj, col_g)
    x2d = matmul(merged, lp["w_o"], F32, "residual", x2d)
    h2 = rmsnorm_bf16(x2d, lp["norm2_g"])
    ff = matmul(h2, lp["w_mlp1"], BF16, "relu2")
    x2d = matmul(ff, lp["w_mlp2"], F32, "residual", x2d)
    return x2d, new_k, new_v, new_state, new_shift, new_conv


def kernel(x_prompt, x_sample, cache_k, cache_v, page_table, state_rwkv, state_shift, state_conv, norm1_g, w_in, q_norm_g, k_norm_g, lambda_q1, lambda_k1, lambda_q2, lambda_k2, subln_g, rwkv_mu, rwkv_w0, rwkv_w2, rwkv_a0, rwkv_a2, rwkv_g2, rwkv_kk, rwkv_ka, rwkv_rk, rwkv_lnx_g, rwkv_lnx_b, conv_w, conv_b, conv_ln_g, conv_ln_b, w_pa, w_pb, w_pc, w_o, norm2_g, w_mlp1, w_mlp2):
    depth = w_in.shape[0]
    b_p, t_p, d_model = x_prompt.shape
    b_s, t_s, _ = x_sample.shape
    assert t_s == 1
    c_r = w_pb.shape[1]
    c_c = w_pc.shape[1]
    past_len = page_table.shape[1] * PAGE_SIZE
    o_r = 3 * A_WIDTH
    o_c = o_r + 3 * c_r + DECAY_RANK + A_RANK + G_RANK
    o_g = o_c + 2 * c_c

    tab_p = _rope_tables(jnp.arange(t_p))
    tab_s = _rope_tables(jnp.full((b_s,), past_len))
    ck = cache_k.reshape(cache_k.shape[0], cache_k.shape[1], PAGE_SIZE, A_WIDTH)
    cv = cache_v.reshape(cache_v.shape[0], cache_v.shape[1], PAGE_SIZE, A_WIDTH)

    pad_rows = lambda w, n: jnp.concatenate([w, jnp.zeros((n - w.shape[0], w.shape[1]), w.dtype)], axis=0)
    row = lambda v: v.reshape(1, -1)

    yp = x_prompt.reshape(b_p * t_p, d_model)
    ys = x_sample.reshape(b_s, d_model)
    outs = [[] for _ in range(10)]
    for l in range(depth):
        wl = w_in[l]
        lp = dict(
            norm1_g=norm1_g[l], norm2_g=norm2_g[l],
            w_in_cat=jnp.concatenate([wl[:, :o_r], wl[:, o_g:], wl[:, o_c:o_g],
                                      _rwkv_pad_cols(wl[:, o_r:o_c], c_r)], axis=1).astype(BF16),
            qk_gain=jnp.stack([q_norm_g[l], k_norm_g[l]]).reshape(2, 1, HEAD_DIM),
            subln_g=subln_g[l],
            rwkv=dict(mu=row(_rwkv_pad_cols(rwkv_mu[l], c_r)), w0=row(rwkv_w0[l]),
                      w2=pad_rows(rwkv_w2[l], LANE).astype(BF16), a0=row(rwkv_a0[l]),
                      a2=pad_rows(rwkv_a2[l], LANE).astype(BF16), g2=pad_rows(rwkv_g2[l], LANE).astype(BF16),
                      kk=row(rwkv_kk[l]), ka=row(rwkv_ka[l])),
            rk=row(rwkv_rk[l]), lnx_g=row(rwkv_lnx_g[l]), lnx_b=row(rwkv_lnx_b[l]),
            conv_w=conv_w[l], conv_b=conv_b[l], conv_ln_g=conv_ln_g[l], conv_ln_b=conv_ln_b[l],
            w_pa=w_pa[l].astype(BF16), w_pb=w_pb[l].astype(BF16), w_pc=w_pc[l].astype(BF16),
            w_o=w_o[l].astype(BF16), w_mlp1=w_mlp1[l].astype(BF16), w_mlp2=w_mlp2[l].astype(BF16),
        )
        lambda_init = 0.8 - 0.6 * math.exp(-0.3 * l)
        lam = (jnp.exp(jnp.sum(lambda_q1[l] * lambda_k1[l])) - jnp.exp(jnp.sum(lambda_q2[l] * lambda_k2[l]))
               + lambda_init).reshape(1).astype(F32)

        yp, k_new, v_new, s_new, sh_new, c_new = _layer(yp, b_p, t_p, lp, lam, lambda_init, tab_p, None)
        outs[0].append(k_new.reshape(b_p, t_p, N_HEADS_A, 2, HEAD_DIM))
        outs[1].append(v_new.reshape(b_p, t_p, N_HEADS_A, 2 * HEAD_DIM))
        outs[4].append(s_new)
        outs[6].append(sh_new)
        outs[8].append(c_new)

        st = dict(cache_k=ck, cache_v=cv, layer=l, page_table=page_table, rwkv=state_rwkv[l],
                  shift=state_shift[l], conv=state_conv[l])
        ys, k_new, v_new, s_new, sh_new, c_new = _layer(ys, b_s, 1, lp, lam, lambda_init, tab_s, st)
        outs[2].append(k_new.reshape(b_s, 1, N_HEADS_A, 2, HEAD_DIM))
        outs[3].append(v_new.reshape(b_s, 1, N_HEADS_A, 2 * HEAD_DIM))
        outs[5].append(s_new)
        outs[7].append(sh_new)
        outs[9].append(c_new)
    return (yp.reshape(b_p, t_p, d_model), ys.reshape(b_s, 1, d_model)) + tuple(jnp.stack(o) for o in outs)
```

```python
import functools
import math

import jax
import jax.numpy as jnp
from jax import lax
from jax.experimental import pallas as pl
from jax.experimental.pallas import tpu as pltpu

F32 = jnp.float32
BF16 = jnp.bfloat16

N_HEADS_A = 8
HEAD_DIM = 128
A_WIDTH = N_HEADS_A * 2 * HEAD_DIM
ROPE_THETA = 10000.0
RWKV_HEAD = 64
DECAY_RANK = 64
A_RANK = 64
G_RANK = 32
CONV_W = 31
PAGE_SIZE = 128
RMS_EPS = 1e-6
LN_EPS = 1e-5
LN_X_EPS = 64e-5

LANE = 128
RWKV_CHUNK = 64
RWKV_PAIRS_PER_STEP = 4
CONV_HALO = 32
NEG_BIG = -1e30
LOG2E = 1.4426950408889634


def _sigmoid(x):
    return 1.0 / (1.0 + jnp.exp(-x))


def _cparams(*sem):
    return pltpu.CompilerParams(dimension_semantics=sem, vmem_limit_bytes=56 * 1024 * 1024)


def _pick(n, cands):
    for c in cands:
        if n % c == 0:
            return c
    raise ValueError(f"no tile for {n}")


def _rmsnorm_kernel(x_ref, g_ref, o_ref):
    x = x_ref[...]
    ms = jnp.mean(x * x, axis=-1, keepdims=True)
    o_ref[...] = (x * lax.rsqrt(ms + RMS_EPS) * g_ref[...]).astype(o_ref.dtype)


def rmsnorm_bf16(x, g):
    m, d = x.shape
    tm = _pick(m, (512, 256, 128, 64, 32, 16))
    return pl.pallas_call(
        _rmsnorm_kernel,
        grid=(m // tm,),
        in_specs=[pl.BlockSpec((tm, d), lambda i: (i, 0)), pl.BlockSpec((1, d), lambda i: (0, 0))],
        out_specs=pl.BlockSpec((tm, d), lambda i: (i, 0)),
        out_shape=jax.ShapeDtypeStruct((m, d), BF16),
        compiler_params=_cparams("parallel"),
        name="rmsnorm",
    )(x, g.reshape(1, d))


def _matmul_kernel(*refs, nk, epilogue):
    if epilogue == "residual":
        x_ref, w_ref, res_ref, o_ref = refs[:4]
        rest = refs[4:]
    else:
        x_ref, w_ref, o_ref = refs[:3]
        res_ref = None
        rest = refs[3:]

    def finish(acc):
        if epilogue == "relu2":
            r = jnp.maximum(acc, 0.0)
            acc = r * r
        elif epilogue == "residual":
            acc = acc + res_ref[...]
        o_ref[...] = acc.astype(o_ref.dtype)

    part = jnp.dot(x_ref[...], w_ref[...], preferred_element_type=F32)
    if nk == 1:
        finish(part)
    else:
        acc_ref = rest[0]
        k = pl.program_id(2)

        @pl.when(k == 0)
        def _():
            acc_ref[...] = part

        @pl.when(k > 0)
        def _():
            acc_ref[...] += part

        @pl.when(k == nk - 1)
        def _():
            finish(acc_ref[...])


def matmul(x, w, out_dtype, epilogue=None, res=None):
    m, kdim = x.shape
    n = w.shape[1]
    tm = _pick(m, (1024, 512, 256, 128, 64, 32, 16))
    tn = _pick(n, (512, 256, 128))
    tk = _pick(kdim, (2048, 1024, 512))
    nk = kdim // tk
    in_specs = [pl.BlockSpec((tm, tk), lambda i, j, k: (i, k)),
                pl.BlockSpec((tk, tn), lambda i, j, k: (k, j))]
    args = [x, w]
    if epilogue == "residual":
        in_specs.append(pl.BlockSpec((tm, tn), lambda i, j, k: (i, j)))
        args.append(res)
    scratch = [pltpu.VMEM((tm, tn), F32)] if nk > 1 else []
    return pl.pallas_call(
        functools.partial(_matmul_kernel, nk=nk, epilogue=epilogue),
        grid=(m // tm, n // tn, nk),
        in_specs=in_specs,
        out_specs=pl.BlockSpec((tm, tn), lambda i, j, k: (i, j)),
        out_shape=jax.ShapeDtypeStruct((m, n), out_dtype),
        scratch_shapes=scratch,
        compiler_params=_cparams("parallel", "parallel", "arbitrary"),
        name="matmul_" + (epilogue or "plain"),
    )(*args)


def _qknorm_rope_kernel(x_ref, g_ref, cos_ref, sin_ref, o_ref):
    g = g_ref[...]
    cosf = cos_ref[...]
    sins = sin_ref[...]
    for grp in range(x_ref.shape[1] // HEAD_DIM):
        sl = slice(grp * HEAD_DIM, (grp + 1) * HEAD_DIM)
        x = x_ref[:, sl]
        ms = jnp.mean(x * x, axis=-1, keepdims=True)
        y = x * lax.rsqrt(ms + RMS_EPS) * g
        o_ref[:, sl] = y * cosf + pltpu.roll(y, HEAD_DIM // 2, 1) * sins


def qknorm_rope(proj, gains, cosf, sins):
    m = proj.shape[0]
    p = cosf.shape[0]
    tm = _pick(math.gcd(m, p), (256, 128, 64, 32, 16, 8))
    npb = p // tm
    return pl.pallas_call(
        _qknorm_rope_kernel,
        grid=(m // tm, 2),
        in_specs=[pl.BlockSpec((tm, A_WIDTH), lambda i, s: (i, s)),
                  pl.BlockSpec((None, 1, HEAD_DIM), lambda i, s: (s, 0, 0)),
                  pl.BlockSpec((tm, HEAD_DIM), lambda i, s: (i % npb, 0)),
                  pl.BlockSpec((tm, HEAD_DIM), lambda i, s: (i % npb, 0))],
        out_specs=pl.BlockSpec((None, tm, A_WIDTH), lambda i, s: (s, i, 0)),
        out_shape=jax.ShapeDtypeStruct((2, m, A_WIDTH), F32),
        compiler_params=_cparams("parallel", "parallel"),
        name="qknorm_rope",
    )(proj, gains, cosf, sins)


def _subln(o, g, scale):
    ms = jnp.mean(o * o, axis=-1, keepdims=True)
    return o * lax.rsqrt(ms + LN_EPS) * g * scale


def _flash_kernel(lam_ref, q_ref, k_ref, v_ref, g_ref, o_ref, qs_ref, m_ref, l_ref, acc_ref, *, out_scale):
    qi = pl.program_id(2)
    ki = pl.program_id(3)
    tq = q_ref.shape[0]
    tk = k_ref.shape[0]

    @pl.when(ki == 0)
    def _():
        qs_ref[...] = (q_ref[...] * (HEAD_DIM ** -0.5 * LOG2E)).astype(BF16)
        m_ref[...] = jnp.full(m_ref.shape, NEG_BIG, F32)
        l_ref[...] = jnp.zeros(l_ref.shape, F32)
        acc_ref[...] = jnp.zeros(acc_ref.shape, F32)

    def update(masked):
        vb = v_ref[...].astype(BF16)
        for mp in range(2):
            sl = slice(mp * HEAD_DIM, (mp + 1) * HEAD_DIM)
            s = lax.dot_general(qs_ref[:, sl], k_ref[:, sl].astype(BF16), (((1,), (1,)), ((), ())),
                                preferred_element_type=F32)
            if masked:
                row = lax.broadcasted_iota(jnp.int32, (tq, tk), 0)
                col = lax.broadcasted_iota(jnp.int32, (tq, tk), 1)
                s = jnp.where(col <= row, s, NEG_BIG)
            m_prev = m_ref[mp]
            m_new = jnp.maximum(m_prev, jnp.max(s, axis=-1, keepdims=True))
            p = jnp.exp2(s - m_new)
            alpha = jnp.exp2(m_prev - m_new)
            l_ref[mp] = alpha * l_ref[mp] + jnp.sum(p, axis=-1, keepdims=True)
            acc_ref[mp] = alpha * acc_ref[mp] + jnp.dot(p.astype(BF16), vb, preferred_element_type=F32)
            m_ref[mp] = m_new

    @pl.when(ki < qi)
    def _():
        update(False)

    @pl.when(ki == qi)
    def _():
        update(True)
        lam = lam_ref[0]
        o = acc_ref[0] / l_ref[0] - lam * (acc_ref[1] / l_ref[1])
        o_ref[...] = _subln(o, g_ref[...], out_scale).astype(o_ref.dtype)


def flash_diff_attention(lam, qk, proj, v_col0, subln_g, batch, seq, out_scale):
    m = batch * seq
    tq = _pick(seq, (512, 256, 128))
    nq = seq // tq
    hw = 2 * HEAD_DIM
    vb0 = v_col0 // hw
    return pl.pallas_call(
        functools.partial(_flash_kernel, out_scale=out_scale),
        grid=(batch, N_HEADS_A, nq, nq),
        in_specs=[pl.BlockSpec(memory_space=pltpu.SMEM),
                  pl.BlockSpec((None, tq, hw), lambda b, h, qi, ki: (0, b * nq + qi, h)),
                  pl.BlockSpec((None, tq, hw), lambda b, h, qi, ki: (1, b * nq + jnp.minimum(ki, qi), h)),
                  pl.BlockSpec((tq, hw), lambda b, h, qi, ki: (b * nq + jnp.minimum(ki, qi), vb0 + h)),
                  pl.BlockSpec((1, hw), lambda b, h, qi, ki: (0, 0))],
        out_specs=pl.BlockSpec((tq, hw), lambda b, h, qi, ki: (b * nq + qi, h)),
        out_shape=jax.ShapeDtypeStruct((m, A_WIDTH), BF16),
        scratch_shapes=[pltpu.VMEM((tq, hw), BF16), pltpu.VMEM((2, tq, 1), F32), pltpu.VMEM((2, tq, 1), F32),
                        pltpu.VMEM((2, tq, hw), F32)],
        compiler_params=_cparams("parallel", "parallel", "parallel", "arbitrary"),
        name="flash_diff_attention",
    )(lam, qk, qk, proj, subln_g.reshape(1, hw))


def _decode_kernel(pt_ref, lam_ref, q_ref, kn_ref, vn_ref, *rest, n_steps, pps, out_scale):
    kc_refs = rest[:pps]
    vc_refs = rest[pps:2 * pps]
    g_ref, o_ref, m_ref, l_ref, acc_ref = rest[2 * pps:]
    p = pl.program_id(1)
    nh = N_HEADS_A
    rows = PAGE_SIZE * nh

    @pl.when(p == 0)
    def _():
        m_ref[...] = jnp.full(m_ref.shape, NEG_BIG, F32)
        l_ref[...] = jnp.zeros(l_ref.shape, F32)
        acc_ref[...] = jnp.zeros(acc_ref.shape, F32)

    q = q_ref[...] * (HEAD_DIM ** -0.5 * LOG2E)
    own_head = (lax.broadcasted_iota(jnp.int32, (nh, rows), 1) % nh
                == lax.broadcasted_iota(jnp.int32, (nh, rows), 0))
    for kc_ref, vc_ref in zip(kc_refs, vc_refs):
        parts = []
        for mp in range(2):
            k_m = kc_ref[pl.ds(mp, rows, stride=2), :].astype(BF16)
            s = lax.dot_general(q[mp * nh:(mp + 1) * nh].astype(BF16), k_m, (((1,), (1,)), ((), ())),
                                preferred_element_type=F32)
            parts.append(jnp.where(own_head, s, NEG_BIG))
        s = jnp.concatenate(parts, axis=0)
        m_prev = m_ref[...]
        m_new = jnp.maximum(m_prev, jnp.max(s, axis=-1, keepdims=True))
        pr = jnp.exp2(s - m_new)
        alpha = jnp.exp2(m_prev - m_new)
        l_ref[...] = alpha * l_ref[...] + jnp.sum(pr, axis=-1, keepdims=True)
        acc_ref[...] = alpha * acc_ref[...] + jnp.dot(pr.astype(BF16), vc_ref[...].astype(BF16),
                                                      preferred_element_type=F32)
        m_ref[...] = m_new

    @pl.when(p == n_steps - 1)
    def _():
        s_new = jnp.sum(q * kn_ref[...], axis=-1, keepdims=True)
        m_prev = m_ref[...]
        m_fin = jnp.maximum(m_prev, s_new)
        p_new = jnp.exp2(s_new - m_fin)
        alpha = jnp.exp2(m_prev - m_fin)
        l_fin = alpha * l_ref[...] + p_new
        vn = vn_ref[...]
        acc = alpha * acc_ref[...] + p_new * jnp.concatenate([vn, vn], axis=0)
        d = acc / l_fin
        o = d[0:nh] - lam_ref[0] * d[nh:2 * nh]
        o_ref[...] = _subln(o, g_ref[...], out_scale)


def paged_diff_attention(lam, qk, proj, v_col0, cache_k, cache_v, layer_idx, page_table, subln_g, out_scale):
    bs, n_pages = page_table.shape
    nh = N_HEADS_A
    hw = 2 * HEAD_DIM
    pps = 2 if n_pages % 2 == 0 else 1
    n_steps = n_pages // pps
    qk_mh = qk.reshape(2, bs, nh, 2, HEAD_DIM).transpose(0, 1, 3, 2, 4).reshape(2, bs, 2 * nh, HEAD_DIM)
    vn = proj[:, v_col0:v_col0 + A_WIDTH].reshape(bs, nh, hw)
    ck = cache_k.reshape(cache_k.shape[0], cache_k.shape[1], PAGE_SIZE * nh * 2, HEAD_DIM)
    cv = cache_v.reshape(cache_v.shape[0], cache_v.shape[1], PAGE_SIZE * nh, hw)

    def page_spec(rows, width, j):
        return pl.BlockSpec((None, None, rows, width),
                            lambda b, p, pt: (layer_idx, pt[b * n_pages + p * pps + j], 0, 0))

    grid_spec = pltpu.PrefetchScalarGridSpec(
        num_scalar_prefetch=1,
        grid=(bs, n_steps),
        in_specs=[pl.BlockSpec(memory_space=pltpu.SMEM),
                  pl.BlockSpec((None, None, 2 * nh, HEAD_DIM), lambda b, p, pt: (0, b, 0, 0)),
                  pl.BlockSpec((None, None, 2 * nh, HEAD_DIM), lambda b, p, pt: (1, b, 0, 0)),
                  pl.BlockSpec((None, nh, hw), lambda b, p, pt: (b, 0, 0))]
                 + [page_spec(PAGE_SIZE * nh * 2, HEAD_DIM, j) for j in range(pps)]
                 + [page_spec(PAGE_SIZE * nh, hw, j) for j in range(pps)]
                 + [pl.BlockSpec((1, hw), lambda b, p, pt: (0, 0))],
        out_specs=pl.BlockSpec((None, nh, hw), lambda b, p, pt: (b, 0, 0)),
        scratch_shapes=[pltpu.VMEM((2 * nh, 1), F32), pltpu.VMEM((2 * nh, 1), F32),
                        pltpu.VMEM((2 * nh, hw), F32)],
    )
    out = pl.pallas_call(
        functools.partial(_decode_kernel, n_steps=n_steps, pps=pps, out_scale=out_scale),
        grid_spec=grid_spec,
        out_shape=jax.ShapeDtypeStruct((bs, nh, hw), F32),
        compiler_params=_cparams("parallel", "arbitrary"),
        name="paged_diff_attention",
    )(page_table.reshape(-1), lam, qk_mh, qk_mh, vn, *([ck] * pps), *([cv] * pps), subln_g.reshape(1, hw))
    return out.reshape(bs, A_WIDTH).astype(BF16)


def _rwkv_pad_cols(x, c_r):
    lead = x.shape[:-1]
    z = lambda n: jnp.zeros(lead + (n,), x.dtype)
    o = 3 * c_r
    return jnp.concatenate([x[..., :o],
                            x[..., o:o + DECAY_RANK], z(LANE - DECAY_RANK),
                            x[..., o + DECAY_RANK:o + DECAY_RANK + A_RANK], z(LANE - A_RANK),
                            x[..., o + DECAY_RANK + A_RANK:], z(LANE - G_RANK),
                            z(LANE)], axis=-1)


def _rwkv_unpad_cols(x, c_r):
    o = 3 * c_r
    return jnp.concatenate([x[..., :o], x[..., o:o + DECAY_RANK],
                            x[..., o + LANE:o + LANE + A_RANK],
                            x[..., o + 2 * LANE:o + 2 * LANE + G_RANK]], axis=-1)


def _rwkv_premix_kernel(pr_ref, prev_ref, mu_ref, w0_ref, w2_ref, a0_ref, a2_ref, g2_ref, kkp_ref, kap_ref,
                        o_ref, *, c_r, seq, single_step):
    i = pl.program_id(0)
    tm = pr_ref.shape[0]
    pr = pr_ref[...]
    if single_step:
        prev = prev_ref[...]
    else:
        rolled = pltpu.roll(pr, 1, 0)
        first = jnp.where((i * tm) % seq == 0, 0.0, 1.0) * prev_ref[7:8, :]
        row = lax.broadcasted_iota(jnp.int32, pr.shape, 0)
        prev = jnp.where(row == 0, first, rolled)
    ps = pr + mu_ref[...] * (prev - pr)
    r = ps[:, 0:c_r]
    kr = ps[:, c_r:2 * c_r]
    vr = ps[:, 2 * c_r:3 * c_r]
    wd = ps[:, 3 * c_r:3 * c_r + LANE]
    ad = ps[:, 3 * c_r + LANE:3 * c_r + 2 * LANE]
    gd = ps[:, 3 * c_r + 2 * LANE:3 * c_r + 3 * LANE]
    z = -(w0_ref[...] + jnp.dot(jnp.tanh(wd).astype(BF16), w2_ref[...], preferred_element_type=F32))
    softplus = jnp.maximum(z, 0.0) + jnp.log(1.0 + jnp.exp(-jnp.abs(z)))
    w_log = -softplus - 0.5
    a = _sigmoid(a0_ref[...] + jnp.dot(ad.astype(BF16), a2_ref[...], preferred_element_type=F32))
    g = jnp.dot(_sigmoid(gd).astype(BF16), g2_ref[...], preferred_element_type=F32)
    o_ref[0] = r
    o_ref[1] = kr * (1.0 + (a - 1.0) * kap_ref[...])
    o_ref[2] = vr
    o_ref[3] = -jnp.exp(w_log)
    o_ref[4] = kr * kkp_ref[...]
    o_ref[5] = a
    o_ref[6] = g


def rwkv_premix(proj, r_col0, prev, p, seq, single_step):
    m = proj.shape[0]
    c_r = p["w0"].shape[1]
    wr = 3 * c_r + 4 * LANE
    cb = r_col0 // wr
    tm = _pick(math.gcd(m, seq) if not single_step else m, (256, 128, 64, 32, 16, 8))
    if single_step:
        prev_spec = pl.BlockSpec((tm, wr), lambda i: (i, 0))
        prev_arg = prev
    else:
        prev_spec = pl.BlockSpec((8, wr), lambda i: (jnp.maximum(i * (tm // 8) - 1, 0), cb))
        prev_arg = proj
    vec = lambda n: pl.BlockSpec((1, n), lambda i: (0, 0))
    mat = lambda: pl.BlockSpec((LANE, c_r), lambda i: (0, 0))
    return pl.pallas_call(
        functools.partial(_rwkv_premix_kernel, c_r=c_r, seq=seq, single_step=single_step),
        grid=(m // tm,),
        in_specs=[pl.BlockSpec((tm, wr), lambda i: (i, cb)), prev_spec, vec(wr),
                  vec(c_r), mat(), vec(c_r), mat(), mat(), vec(c_r), vec(c_r)],
        out_specs=pl.BlockSpec((7, tm, c_r), lambda i: (0, i, 0)),
        out_shape=jax.ShapeDtypeStruct((7, m, c_r), F32),
        compiler_params=_cparams("parallel"),
        name="rwkv_premix",
    )(proj, prev_arg, p["mu"], p["w0"], p["w2"], p["a0"], p["a2"], p["g2"], p["kk"], p["ka"])


def _split3(x):
    hi = x.astype(BF16)
    r1 = x - hi.astype(F32)
    mid = r1.astype(BF16)
    lo = (r1 - mid.astype(F32)).astype(BF16)
    return hi, mid, lo


def _halfsum(x, m0):
    s0 = jnp.sum(jnp.where(m0, x, 0.0), axis=-1, keepdims=True)
    s1 = jnp.sum(jnp.where(m0, 0.0, x), axis=-1, keepdims=True)
    return jnp.where(m0, s0, s1)


def _bdot(a, b):
    return jnp.dot(a.astype(BF16), b.astype(BF16), preferred_element_type=F32)


def _bdot_nt(a, b):
    return lax.dot_general(a.astype(BF16), b.astype(BF16), (((1,), (1,)), ((), ())),
                           preferred_element_type=F32)


def _bdot_tn(a, b):
    return lax.dot_general(a.astype(BF16), b.astype(BF16), (((0,), (0,)), ((), ())),
                           preferred_element_type=F32)


def _same_head_mask():
    vrow = lax.broadcasted_iota(jnp.int32, (LANE, LANE), 0)
    kcol = lax.broadcasted_iota(jnp.int32, (LANE, LANE), 1)
    return (vrow < RWKV_HEAD) == (kcol < RWKV_HEAD)


def _rwkv_chunk_kernel(x_ref, rk_ref, rt2_ref, yvgb_ref, m2_ref, z_ref, ec_ref):
    C = x_ref.shape[1]
    N = RWKV_HEAD
    pairs = range(x_ref.shape[2] // LANE)
    sls = [slice(j * LANE, (j + 1) * LANE) for j in pairs]
    m0 = lax.broadcasted_iota(jnp.int32, (C, LANE), 1) < N
    m1 = jnp.logical_not(m0)
    trow = lax.broadcasted_iota(jnp.int32, (C, C), 0)
    tcol = lax.broadcasted_iota(jnp.int32, (C, C), 1)
    strict = tcol < trow
    incl = tcol <= trow
    tri = jnp.where(incl, 1.0, 0.0).astype(BF16)
    zeros = jnp.zeros((C, LANE), F32)
    same_head = _same_head_mask()

    lw = [x_ref[3, :, sl] for sl in sls]
    split = [_split3(x) for x in lw]
    cs = [jnp.dot(tri, s[0], preferred_element_type=F32) + jnp.dot(tri, s[1], preferred_element_type=F32)
          + jnp.dot(tri, s[2], preferred_element_type=F32) for s in split]

    r = [x_ref[0, :, sl] for sl in sls]
    k = [x_ref[1, :, sl] for sl in sls]
    v = [x_ref[2, :, sl] for sl in sls]
    a = [x_ref[5, :, sl] for sl in sls]
    kk = []
    for sl in sls:
        kr = x_ref[4, :, sl]
        kk.append(kr * lax.rsqrt(_halfsum(kr * kr, m0) + 1e-12))
    cs_end = [c[C - 1:C, :] for c in cs]
    al, rt, b_end, k_end, amat = [], [], [], [], []
    for j in pairs:
        e_neg = jnp.exp(-cs[j])
        al_j = -kk[j] * jnp.exp(cs[j] - lw[j])
        rt_j = r[j] * jnp.exp(cs[j])
        kb = kk[j] * a[j]
        e_end = jnp.exp(cs_end[j] - cs[j])
        al.append(al_j)
        rt.append(rt_j)
        b_end.append(kb * e_end)
        k_end.append(k[j] * e_end)
        amat.append(_bdot_nt(
            jnp.concatenate([jnp.where(m0, al_j, 0.0), jnp.where(m0, rt_j, 0.0),
                             jnp.where(m1, al_j, 0.0), jnp.where(m1, rt_j, 0.0)], axis=0),
            jnp.concatenate([kb * e_neg, k[j] * e_neg], axis=0)))

    heads = [(j, h) for j in pairs for h in range(2)]
    a_ab, a_rbk, x = {}, {}, {}
    for j, h in heads:
        base = 2 * C * h
        A = amat[j]
        a_ab[j, h] = jnp.where(strict, A[base:base + C, 0:C], 0.0)
        a_ak = jnp.where(strict, A[base:base + C, C:2 * C], 0.0)
        a_rbk[j, h] = jnp.concatenate([jnp.where(incl, A[base + C:base + 2 * C, 0:C], 0.0),
                                       jnp.where(incl, A[base + C:base + 2 * C, C:2 * C], 0.0)], axis=1)
        x[j, h] = jnp.concatenate([jnp.where(m0 if h == 0 else m1, al[j], 0.0), _bdot(a_ak, v[j])], axis=1)
    n = 1
    while n < C:
        for jh in heads:
            x[jh] = x[jh] + _bdot(a_ab[jh], x[jh])
        n *= 2
        if n < C:
            for jh in heads:
                a_ab[jh] = _bdot(a_ab[jh], a_ab[jh])

    uv = [jnp.where(m0, x[j, 0][:, LANE:], x[j, 1][:, LANE:]) for j in pairs]
    yh = {}
    for j, h in heads:
        yh[j, h] = _bdot(a_rbk[j, h],
                         jnp.concatenate([jnp.concatenate([x[j, h][:, :LANE], uv[j]], axis=1),
                                          jnp.concatenate([zeros, v[j]], axis=1)], axis=0))
    for j in pairs:
        sl = sls[j]
        al2 = x[j, 0][:, :LANE] + x[j, 1][:, :LANE]
        m2 = jnp.where(same_head, _bdot_tn(al2, b_end[j]), 0.0)
        z = jnp.where(same_head, _bdot_tn(jnp.concatenate([uv[j], v[j]], axis=0),
                                          jnp.concatenate([b_end[j], k_end[j]], axis=0)), 0.0)
        rt2 = rt[j] + yh[j, 0][:, :LANE] + yh[j, 1][:, :LANE]
        yv = jnp.where(m0, yh[j, 0][:, LANE:], yh[j, 1][:, LANE:])
        rt2_ref[:, sl] = rt2.astype(BF16)
        yvgb_ref[0, :, sl] = yv
        yvgb_ref[1, :, sl] = x_ref[6, :, sl]
        yvgb_ref[2, :, sl] = _halfsum(r[j] * k[j] * rk_ref[:, sl], m0) * v[j]
        m2_ref[j] = m2.astype(BF16)
        z_ref[j] = z
        ec_ref[:, sl] = jnp.exp(cs_end[j])


def _rwkv_state_kernel(rt2_ref, yvgb_ref, m2_ref, z_ref, ec_ref, lng_ref, lnb_ref, o_ref, s_out_ref, s_ref):
    c = pl.program_id(1)
    C = rt2_ref.shape[0]
    N = RWKV_HEAD

    @pl.when(c == 0)
    def _():
        s_ref[...] = jnp.zeros(s_ref.shape, F32)

    m0 = lax.broadcasted_iota(jnp.int32, (C, LANE), 1) < N
    for j in range(s_ref.shape[0]):
        sl = slice(j * LANE, (j + 1) * LANE)
        S = s_ref[j]
        Sb = S.astype(BF16)
        y = lax.dot_general(rt2_ref[:, sl], Sb, (((1,), (1,)), ((), ())),
                            preferred_element_type=F32) + yvgb_ref[0, :, sl]
        s_new = S * ec_ref[:, sl] + jnp.dot(Sb, m2_ref[j], preferred_element_type=F32) + z_ref[j]
        s_ref[j] = s_new
        s_out_ref[j] = s_new
        mu = _halfsum(y, m0) * (1.0 / N)
        d = y - mu
        var = _halfsum(d * d, m0) * (1.0 / N)
        yn = d * lax.rsqrt(var + LN_X_EPS) * lng_ref[:, sl] + lnb_ref[:, sl]
        o_ref[:, sl] = ((yn + yvgb_ref[2, :, sl]) * yvgb_ref[1, :, sl]).astype(o_ref.dtype)


def rwkv_scan(x7, rk, lnx_g, lnx_b, batch, seq):
    m, c_r = x7.shape[1], x7.shape[2]
    C = RWKV_CHUNK
    nc = seq // C
    nct = m // C
    npair = c_r // LANE
    pp = _pick(npair, (RWKV_PAIRS_PER_STEP, 2, 1))
    wl = pp * LANE
    rt2, yvgb, m2, z, ec = pl.pallas_call(
        _rwkv_chunk_kernel,
        grid=(nct, npair // pp),
        in_specs=[pl.BlockSpec((7, C, wl), lambda i, j: (0, i, j)),
                  pl.BlockSpec((1, wl), lambda i, j: (0, j))],
        out_specs=[pl.BlockSpec((C, wl), lambda i, j: (i, j)),
                   pl.BlockSpec((3, C, wl), lambda i, j: (0, i, j)),
                   pl.BlockSpec((None, pp, LANE, LANE), lambda i, j: (i, j, 0, 0)),
                   pl.BlockSpec((None, pp, LANE, LANE), lambda i, j: (i, j, 0, 0)),
                   pl.BlockSpec((None, 1, wl), lambda i, j: (i, 0, j))],
        out_shape=[jax.ShapeDtypeStruct((m, c_r), BF16),
                   jax.ShapeDtypeStruct((3, m, c_r), F32),
                   jax.ShapeDtypeStruct((nct, npair, LANE, LANE), BF16),
                   jax.ShapeDtypeStruct((nct, npair, LANE, LANE), F32),
                   jax.ShapeDtypeStruct((nct, 1, c_r), F32)],
        compiler_params=_cparams("parallel", "parallel"),
        name="rwkv_chunk",
    )(x7, rk)
    vec = lambda: pl.BlockSpec((1, c_r), lambda b, c: (0, 0))
    return pl.pallas_call(
        _rwkv_state_kernel,
        grid=(batch, nc),
        in_specs=[pl.BlockSpec((C, c_r), lambda b, c: (b * nc + c, 0)),
                  pl.BlockSpec((3, C, c_r), lambda b, c: (0, b * nc + c, 0)),
                  pl.BlockSpec((None, npair, LANE, LANE), lambda b, c: (b * nc + c, 0, 0, 0)),
                  pl.BlockSpec((None, npair, LANE, LANE), lambda b, c: (b * nc + c, 0, 0, 0)),
                  pl.BlockSpec((None, 1, c_r), lambda b, c: (b * nc + c, 0, 0)),
                  vec(), vec()],
        out_specs=[pl.BlockSpec((C, c_r), lambda b, c: (b * nc + c, 0)),
                   pl.BlockSpec((None, npair, LANE, LANE), lambda b, c: (b, 0, 0, 0))],
        out_shape=[jax.ShapeDtypeStruct((m, c_r), BF16),
                   jax.ShapeDtypeStruct((batch, npair, LANE, LANE), F32)],
        scratch_shapes=[pltpu.VMEM((npair, LANE, LANE), F32)],
        compiler_params=_cparams("parallel", "arbitrary"),
        name="rwkv_state",
    )(rt2, yvgb, m2, z, ec, lnx_g, lnx_b)


def _rwkv_step_kernel(s_ref, x_ref, rk_ref, lng_ref, lnb_ref, s_out_ref, o_ref):
    N = RWKV_HEAD
    S = s_ref[...]
    r = x_ref[0]
    k = x_ref[1]
    v = x_ref[2]
    w = jnp.exp(x_ref[3])
    kk = x_ref[4]
    a = x_ref[5]
    g = x_ref[6]
    kk = kk * lax.rsqrt(jnp.sum(kk * kk, axis=-1, keepdims=True) + 1e-12)
    eye = (lax.broadcasted_iota(jnp.int32, (1, N, N), 1) == lax.broadcasted_iota(jnp.int32, (1, N, N), 2))
    to_col = lambda row: jnp.sum(jnp.where(eye, row, 0.0), axis=-1, keepdims=True)
    to_row = lambda col: jnp.sum(jnp.where(eye, col, 0.0), axis=-2, keepdims=True)
    sk = -jnp.sum(S * kk, axis=-1, keepdims=True)
    s_new = S * w + sk * (kk * a) + to_col(v) * k
    s_out_ref[...] = s_new
    y = to_row(jnp.sum(s_new * r, axis=-1, keepdims=True))
    mu = jnp.mean(y, axis=-1, keepdims=True)
    d = y - mu
    var = jnp.mean(d * d, axis=-1, keepdims=True)
    yn = d * lax.rsqrt(var + LN_X_EPS) * lng_ref[...] + lnb_ref[...]
    bonus = jnp.sum(r * k * rk_ref[...], axis=-1, keepdims=True) * v
    o_ref[...] = (yn + bonus) * g


def rwkv_step(state, x7, rk, lnx_g, lnx_b):
    bsz, nh, N, _ = state.shape
    bb = _pick(bsz, (16, 8, 4, 2, 1))
    x5 = x7.reshape(7, bsz, nh, 1, N)
    pv = lambda t: t.reshape(nh, 1, N)
    vec = lambda: pl.BlockSpec((None, 1, N), lambda i, h: (h, 0, 0))
    s_new, o = pl.pallas_call(
        _rwkv_step_kernel,
        grid=(bsz // bb, nh),
        in_specs=[pl.BlockSpec((bb, None, N, N), lambda i, h: (i, h, 0, 0)),
                  pl.BlockSpec((7, bb, None, 1, N), lambda i, h: (0, i, h, 0, 0)),
                  vec(), vec(), vec()],
        out_specs=[pl.BlockSpec((bb, None, N, N), lambda i, h: (i, h, 0, 0)),
                   pl.BlockSpec((bb, None, 1, N), lambda i, h: (i, h, 0, 0))],
        out_shape=[jax.ShapeDtypeStruct(state.shape, F32),
                   jax.ShapeDtypeStruct((bsz, nh, 1, N), F32)],
        compiler_params=_cparams("parallel", "parallel"),
        name="rwkv_step",
    )(state, x5, pv(rk), pv(lnx_g), pv(lnx_b))
    return s_new, o.reshape(bsz, nh * N)


def _ln_silu(y, g, b):
    mu = jnp.mean(y, axis=-1, keepdims=True)
    d = y - mu
    var = jnp.mean(d * d, axis=-1, keepdims=True)
    z = d * lax.rsqrt(var + LN_EPS) * g + b
    return z * _sigmoid(z)


def _conv_seq_kernel(cu_ref, halo_ref, w_ref, b_ref, lg_ref, lb_ref, o_ref, tail_ref, full_ref, *, seq):
    i = pl.program_id(0)
    tm = cu_ref.shape[0]
    c = cu_ref.shape[1] // 2
    glu = lambda ref: ref[:, 0:c] * _sigmoid(ref[:, c:2 * c])
    keep = jnp.where((i * tm) % seq == 0, 0.0, 1.0)
    full_ref[0:CONV_HALO, :] = glu(halo_ref) * keep
    full_ref[CONV_HALO:CONV_HALO + tm, :] = glu(cu_ref)
    acc = jnp.zeros((tm, c), F32) + b_ref[...]
    off = CONV_HALO - (CONV_W - 1)
    for j in range(CONV_W):
        acc = acc + w_ref[j:j + 1, :] * full_ref[off + j:off + j + tm, :]
    o_ref[...] = _ln_silu(acc, lg_ref[...], lb_ref[...]).astype(o_ref.dtype)
    tail_ref[...] = full_ref[tm:tm + CONV_HALO, :]


def conv_seq(proj, c_col0, conv_w, conv_b, ln_g, ln_b, batch, seq):
    m = proj.shape[0]
    c = conv_w.shape[1]
    tm = _pick(seq, (256, 128, 64, 32))
    cb = c_col0 // (2 * c)
    hb = tm // CONV_HALO
    vec = lambda: pl.BlockSpec((1, c), lambda i: (0, 0))
    return pl.pallas_call(
        functools.partial(_conv_seq_kernel, seq=seq),
        grid=(m // tm,),
        in_specs=[pl.BlockSpec((tm, 2 * c), lambda i: (i, cb)),
                  pl.BlockSpec((CONV_HALO, 2 * c), lambda i: (jnp.maximum(i * hb - 1, 0), cb)),
                  pl.BlockSpec((CONV_W, c), lambda i: (0, 0)), vec(), vec(), vec()],
        out_specs=[pl.BlockSpec((tm, c), lambda i: (i, 0)),
                   pl.BlockSpec((None, CONV_HALO, c), lambda i: ((i * tm) // seq, 0, 0))],
        out_shape=[jax.ShapeDtypeStruct((m, c), BF16),
                   jax.ShapeDtypeStruct((batch, CONV_HALO, c), F32)],
        scratch_shapes=[pltpu.VMEM((CONV_HALO + tm, c), F32)],
        compiler_params=_cparams("arbitrary"),
        name="conv_seq",
    )(proj, proj, conv_w, conv_b.reshape(1, c), ln_g.reshape(1, c), ln_b.reshape(1, c))


def _conv_step_kernel(cu_ref, cp_ref, w_ref, b_ref, lg_ref, lb_ref, o_ref, cn_ref):
    c = cu_ref.shape[1] // 2
    u = cu_ref[:, 0:c] * _sigmoid(cu_ref[:, c:2 * c])
    acc = b_ref[...] + w_ref[CONV_W - 1:CONV_W, :] * u
    for j in range(CONV_W - 1):
        acc = acc + w_ref[j:j + 1, :] * cp_ref[:, j, :]
    o_ref[...] = _ln_silu(acc, lg_ref[...], lb_ref[...]).astype(o_ref.dtype)
    cn_ref[:, 0:CONV_W - 2, :] = cp_ref[:, 1:CONV_W - 1, :]
    cn_ref[:, CONV_W - 2, :] = u


def conv_step(proj, c_col0, conv_prev, conv_w, conv_b, ln_g, ln_b):
    bsz = proj.shape[0]
    c = conv_w.shape[1]
    bb = _pick(bsz, (8,))
    cb = c_col0 // (2 * c)
    vec = lambda: pl.BlockSpec((1, c), lambda i: (0, 0))
    return pl.pallas_call(
        _conv_step_kernel,
        grid=(bsz // bb,),
        in_specs=[pl.BlockSpec((bb, 2 * c), lambda i: (i, cb)),
                  pl.BlockSpec((bb, CONV_W - 1, c), lambda i: (i, 0, 0)),
                  pl.BlockSpec((CONV_W, c), lambda i: (0, 0)), vec(), vec(), vec()],
        out_specs=[pl.BlockSpec((bb, c), lambda i: (i, 0)),
                   pl.BlockSpec((bb, CONV_W - 1, c), lambda i: (i, 0, 0))],
        out_shape=[jax.ShapeDtypeStruct((bsz, c), BF16),
                   jax.ShapeDtypeStruct((bsz, CONV_W - 1, c), F32)],
        compiler_params=_cparams("parallel"),
        name="conv_step",
    )(proj, conv_prev, conv_w, conv_b.reshape(1, c), ln_g.reshape(1, c), ln_b.reshape(1, c))


def _merge_kernel(xa_ref, xb_ref, xc_ref, wa_ref, wb_ref, wc_ref, ga_ref, gb_ref, gc_ref, o_ref):
    m = _sigmoid(ga_ref[...]) * jnp.dot(xa_ref[...], wa_ref[...], preferred_element_type=F32)
    m = m + _sigmoid(gb_ref[...]) * jnp.dot(xb_ref[...], wb_ref[...], preferred_element_type=F32)
    m = m + _sigmoid(gc_ref[...]) * jnp.dot(xc_ref[...], wc_ref[...], preferred_element_type=F32)
    o_ref[...] = m.astype(o_ref.dtype)


def gated_merge(o_a, o_b, o_c, w_pa, w_pb, w_pc, proj, g_col0):
    m = o_a.shape[0]
    d = w_pa.shape[1]
    tm = _pick(m, (512, 256, 128, 64, 32, 16))
    tn = _pick(math.gcd(d, g_col0), (512, 256, 128))
    gb0 = g_col0 // tn
    nb = d // tn
    xs = lambda kd: pl.BlockSpec((tm, kd), lambda i, j: (i, 0))
    ws = lambda kd: pl.BlockSpec((kd, tn), lambda i, j: (0, j))
    gs = lambda br: pl.BlockSpec((tm, tn), lambda i, j: (i, gb0 + br * nb + j))
    return pl.pallas_call(
        _merge_kernel,
        grid=(m // tm, nb),
        in_specs=[xs(o_a.shape[1]), xs(o_b.shape[1]), xs(o_c.shape[1]),
                  ws(w_pa.shape[0]), ws(w_pb.shape[0]), ws(w_pc.shape[0]), gs(0), gs(1), gs(2)],
        out_specs=pl.BlockSpec((tm, tn), lambda i, j: (i, j)),
        out_shape=jax.ShapeDtypeStruct((m, d), BF16),
        compiler_params=_cparams("parallel", "parallel"),
        name="gated_merge",
    )(o_a, o_b, o_c, w_pa, w_pb, w_pc, proj, proj, proj)


def _rope_tables(pos):
    half = HEAD_DIM // 2
    freqs = ROPE_THETA ** (-jnp.arange(half, dtype=F32) * (2.0 / HEAD_DIM))
    ang = pos.astype(F32)[:, None] * freqs[None, :]
    cos, sin = jnp.cos(ang), jnp.sin(ang)
    return jnp.concatenate([cos, cos], axis=-1), jnp.concatenate([-sin, sin], axis=-1)


def _layer(x2d, batch, seq, lp, lam, lambda_init, tables, sample_state):
    d_model = x2d.shape[1]
    c_r = lp["w_pb"].shape[0]
    c_c = lp["w_pc"].shape[0]
    col_v = 2 * A_WIDTH
    col_g = 3 * A_WIDTH
    col_c = col_g + 3 * d_model
    col_r = col_c + 2 * c_c
    out_scale = 1.0 - lambda_init

    h = rmsnorm_bf16(x2d, lp["norm1_g"])
    proj = matmul(h, lp["w_in_cat"], F32)

    qk = qknorm_rope(proj, lp["qk_gain"], *tables)
    if sample_state is None:
        o_a = flash_diff_attention(lam, qk, proj, col_v, lp["subln_g"], batch, seq, out_scale)
    else:
        o_a = paged_diff_attention(lam, qk, proj, col_v, sample_state["cache_k"], sample_state["cache_v"],
                                   sample_state["layer"], sample_state["page_table"], lp["subln_g"], out_scale)
    new_k = qk[1]
    new_v = proj[:, col_v:col_v + A_WIDTH]

    pr_pad = proj[:, col_r:]
    if sample_state is None:
        x7 = rwkv_premix(proj, col_r, None, lp["rwkv"], seq, False)
        o_b, pair_state = rwkv_scan(x7, lp["rk"], lp["lnx_g"], lp["lnx_b"], batch, seq)
        nh = c_r // RWKV_HEAD
        ps = pair_state.reshape(batch, nh // 2, 2, RWKV_HEAD, 2, RWKV_HEAD)
        new_state = jnp.stack([ps[:, :, 0, :, 0, :], ps[:, :, 1, :, 1, :]], axis=2).reshape(
            batch, nh, RWKV_HEAD, RWKV_HEAD)
        new_shift = _rwkv_unpad_cols(pr_pad.reshape(batch, seq, -1)[:, -1], c_r)
    else:
        x7 = rwkv_premix(proj, col_r, _rwkv_pad_cols(sample_state["shift"], c_r), lp["rwkv"], seq, True)
        new_state, o_b = rwkv_step(sample_state["rwkv"], x7, lp["rk"], lp["lnx_g"], lp["lnx_b"])
        o_b = o_b.astype(BF16)
        new_shift = _rwkv_unpad_cols(pr_pad, c_r)

    if sample_state is None:
        o_c, tail = conv_seq(proj, col_c, lp["conv_w"], lp["conv_b"], lp["conv_ln_g"], lp["conv_ln_b"], batch, seq)
        new_conv = tail[:, CONV_HALO - (CONV_W - 1):]
    else:
        o_c, new_conv = conv_step(proj, col_c, sample_state["conv"], lp["conv_w"], lp["conv_b"],
                                  lp["conv_ln_g"], lp["conv_ln_b"])

    merged = gated_merge(o_a, o_b, o_c, lp["w_pa"], lp["w_pb"], lp["w_pc"], proj, col_g)
    x2d = matmul(merged, lp["w_o"], F32, "residual", x2d)
    h2 = rmsnorm_bf16(x2d, lp["norm2_g"])
    ff = matmul(h2, lp["w_mlp1"], BF16, "relu2")
    x2d = matmul(ff, lp["w_mlp2"], F32, "residual", x2d)
    return x2d, new_k, new_v, new_state, new_shift, new_conv


def kernel(x_prompt, x_sample, cache_k, cache_v, page_table, state_rwkv, state_shift, state_conv, norm1_g, w_in, q_norm_g, k_norm_g, lambda_q1, lambda_k1, lambda_q2, lambda_k2, subln_g, rwkv_mu, rwkv_w0, rwkv_w2, rwkv_a0, rwkv_a2, rwkv_g2, rwkv_kk, rwkv_ka, rwkv_rk, rwkv_lnx_g, rwkv_lnx_b, conv_w, conv_b, conv_ln_g, conv_ln_b, w_pa, w_pb, w_pc, w_o, norm2_g, w_mlp1, w_mlp2):
    depth = w_in.shape[0]
    b_p, t_p, d_model = x_prompt.shape
    b_s, t_s, _ = x_sample.shape
    assert t_s == 1
    c_r = w_pb.shape[1]
    c_c = w_pc.shape[1]
    past_len = page_table.shape[1] * PAGE_SIZE
    o_r = 3 * A_WIDTH
    o_c = o_r + 3 * c_r + DECAY_RANK + A_RANK + G_RANK
    o_g = o_c + 2 * c_c

    tab_p = _rope_tables(jnp.arange(t_p))
    tab_s = _rope_tables(jnp.full((b_s,), past_len))

    pad_rows = lambda w, n: jnp.concatenate([w, jnp.zeros((n - w.shape[0], w.shape[1]), w.dtype)], axis=0)
    row = lambda v: v.reshape(1, -1)

    yp = x_prompt.reshape(b_p * t_p, d_model)
    ys = x_sample.reshape(b_s, d_model)
    outs = [[] for _ in range(10)]
    for l in range(depth):
        wl = w_in[l]
        lp = dict(
            norm1_g=norm1_g[l], norm2_g=norm2_g[l],
            w_in_cat=jnp.concatenate([wl[:, :o_r], wl[:, o_g:], wl[:, o_c:o_g],
                                      _rwkv_pad_cols(wl[:, o_r:o_c], c_r)], axis=1).astype(BF16),
            qk_gain=jnp.stack([q_norm_g[l], k_norm_g[l]]).reshape(2, 1, HEAD_DIM),
            subln_g=subln_g[l],
            rwkv=dict(mu=row(_rwkv_pad_cols(rwkv_mu[l], c_r)), w0=row(rwkv_w0[l]),
                      w2=pad_rows(rwkv_w2[l], LANE).astype(BF16), a0=row(rwkv_a0[l]),
                      a2=pad_rows(rwkv_a2[l], LANE).astype(BF16), g2=pad_rows(rwkv_g2[l], LANE).astype(BF16),
                      kk=row(rwkv_kk[l]), ka=row(rwkv_ka[l])),
            rk=row(rwkv_rk[l]), lnx_g=row(rwkv_lnx_g[l]), lnx_b=row(rwkv_lnx_b[l]),
            conv_w=conv_w[l], conv_b=conv_b[l], conv_ln_g=conv_ln_g[l], conv_ln_b=conv_ln_b[l],
            w_pa=w_pa[l].astype(BF16), w_pb=w_pb[l].astype(BF16), w_pc=w_pc[l].astype(BF16),
            w_o=w_o[l].astype(BF16), w_mlp1=w_mlp1[l].astype(BF16), w_mlp2=w_mlp2[l].astype(BF16),
        )
        lambda_init = 0.8 - 0.6 * math.exp(-0.3 * l)
        lam = (jnp.exp(jnp.sum(lambda_q1[l] * lambda_k1[l])) - jnp.exp(jnp.sum(lambda_q2[l] * lambda_k2[l]))
               + lambda_init).reshape(1).astype(F32)

        yp, k_new, v_new, s_new, sh_new, c_new = _layer(yp, b_p, t_p, lp, lam, lambda_init, tab_p, None)
        outs[0].append(k_new.reshape(b_p, t_p, N_HEADS_A, 2, HEAD_DIM))
        outs[1].append(v_new.reshape(b_p, t_p, N_HEADS_A, 2 * HEAD_DIM))
        outs[4].append(s_new)
        outs[6].append(sh_new)
        outs[8].append(c_new)

        st = dict(cache_k=cache_k, cache_v=cache_v, layer=l, page_table=page_table, rwkv=state_rwkv[l],
                  shift=state_shift[l], conv=state_conv[l])
        ys, k_new, v_new, s_new, sh_new, c_new = _layer(ys, b_s, 1, lp, lam, lambda_init, tab_s, st)
        outs[2].append(k_new.reshape(b_s, 1, N_HEADS_A, 2, HEAD_DIM))
        outs[3].append(v_new.reshape(b_s, 1, N_HEADS_A, 2 * HEAD_DIM))
        outs[5].append(s_new)
        outs[7].append(sh_new)
        outs[9].append(c_new)
    return (yp.reshape(b_p, t_p, d_model), ys.reshape(b_s, 1, d_model)) + tuple(jnp.stack(o) for o in outs)
```

```python
import functools
import math

import jax
import jax.numpy as jnp
from jax import lax
from jax.experimental import pallas as pl
from jax.experimental.pallas import tpu as pltpu

F32 = jnp.float32
BF16 = jnp.bfloat16

N_HEADS_A = 8
HEAD_DIM = 128
A_WIDTH = N_HEADS_A * 2 * HEAD_DIM
ROPE_THETA = 10000.0
RWKV_HEAD = 64
DECAY_RANK = 64
A_RANK = 64
G_RANK = 32
CONV_W = 31
PAGE_SIZE = 128
RMS_EPS = 1e-6
LN_EPS = 1e-5
LN_X_EPS = 64e-5

LANE = 128
RWKV_CHUNK = 64
RWKV_PAIRS_PER_STEP = 4
DECODE_PAGES_PER_STEP = 8
CONV_HALO = 32
NEG_BIG = -1e30
LOG2E = 1.4426950408889634


def _sigmoid(x):
    return 1.0 / (1.0 + jnp.exp(-x))


def _cparams(*sem):
    return pltpu.CompilerParams(dimension_semantics=sem, vmem_limit_bytes=56 * 1024 * 1024)


def _pick(n, cands):
    for c in cands:
        if n % c == 0:
            return c
    raise ValueError(f"no tile for {n}")


def _rmsnorm_kernel(x_ref, g_ref, o_ref):
    x = x_ref[...]
    ms = jnp.mean(x * x, axis=-1, keepdims=True)
    o_ref[...] = (x * lax.rsqrt(ms + RMS_EPS) * g_ref[...]).astype(o_ref.dtype)


def rmsnorm_bf16(x, g):
    m, d = x.shape
    tm = _pick(m, (512, 256, 128, 64, 32, 16))
    return pl.pallas_call(
        _rmsnorm_kernel,
        grid=(m // tm,),
        in_specs=[pl.BlockSpec((tm, d), lambda i: (i, 0)), pl.BlockSpec((1, d), lambda i: (0, 0))],
        out_specs=pl.BlockSpec((tm, d), lambda i: (i, 0)),
        out_shape=jax.ShapeDtypeStruct((m, d), BF16),
        compiler_params=_cparams("parallel"),
        name="rmsnorm",
    )(x, g.reshape(1, d))


def _matmul_kernel(*refs, nk, epilogue):
    if epilogue == "residual":
        x_ref, w_ref, res_ref, o_ref = refs[:4]
        rest = refs[4:]
    else:
        x_ref, w_ref, o_ref = refs[:3]
        res_ref = None
        rest = refs[3:]

    def finish(acc):
        if epilogue == "relu2":
            r = jnp.maximum(acc, 0.0)
            acc = r * r
        elif epilogue == "residual":
            acc = acc + res_ref[...]
        o_ref[...] = acc.astype(o_ref.dtype)

    part = jnp.dot(x_ref[...], w_ref[...], preferred_element_type=F32)
    if nk == 1:
        finish(part)
    else:
        acc_ref = rest[0]
        k = pl.program_id(2)

        @pl.when(k == 0)
        def _():
            acc_ref[...] = part

        @pl.when(k > 0)
        def _():
            acc_ref[...] += part

        @pl.when(k == nk - 1)
        def _():
            finish(acc_ref[...])


def matmul(x, w, out_dtype, epilogue=None, res=None, layer=None):
    m, kdim = x.shape
    n = w.shape[-1]
    tm = _pick(m, (1024, 512, 256, 128, 64, 32, 16))
    tn = _pick(n, (1024, 512, 256, 128))
    tk = _pick(kdim, (2048, 1024, 512))
    nk = kdim // tk
    if layer is None:
        w_spec = pl.BlockSpec((tk, tn), lambda i, j, k: (k, j))
    else:
        w_spec = pl.BlockSpec((None, tk, tn), lambda i, j, k: (layer, k, j))
    in_specs = [pl.BlockSpec((tm, tk), lambda i, j, k: (i, k)), w_spec]
    args = [x, w]
    if epilogue == "residual":
        in_specs.append(pl.BlockSpec((tm, tn), lambda i, j, k: (i, j)))
        args.append(res)
    scratch = [pltpu.VMEM((tm, tn), F32)] if nk > 1 else []
    return pl.pallas_call(
        functools.partial(_matmul_kernel, nk=nk, epilogue=epilogue),
        grid=(m // tm, n // tn, nk),
        in_specs=in_specs,
        out_specs=pl.BlockSpec((tm, tn), lambda i, j, k: (i, j)),
        out_shape=jax.ShapeDtypeStruct((m, n), out_dtype),
        scratch_shapes=scratch,
        compiler_params=_cparams("parallel", "parallel", "arbitrary"),
        name="matmul_" + (epilogue or "plain"),
    )(*args)


def _qknorm_rope_kernel(x_ref, g_ref, cos_ref, sin_ref, o_ref):
    g = g_ref[...]
    cosf = cos_ref[...]
    sins = sin_ref[...]
    for grp in range(x_ref.shape[1] // HEAD_DIM):
        sl = slice(grp * HEAD_DIM, (grp + 1) * HEAD_DIM)
        x = x_ref[:, sl]
        ms = jnp.mean(x * x, axis=-1, keepdims=True)
        y = x * lax.rsqrt(ms + RMS_EPS) * g
        o_ref[:, sl] = y * cosf + pltpu.roll(y, HEAD_DIM // 2, 1) * sins


def qknorm_rope(proj, gains, cosf, sins):
    m = proj.shape[0]
    p = cosf.shape[0]
    tm = _pick(math.gcd(m, p), (256, 128, 64, 32, 16, 8))
    npb = p // tm
    return pl.pallas_call(
        _qknorm_rope_kernel,
        grid=(m // tm, 2),
        in_specs=[pl.BlockSpec((tm, A_WIDTH), lambda i, s: (i, s)),
                  pl.BlockSpec((None, 1, HEAD_DIM), lambda i, s: (s, 0, 0)),
                  pl.BlockSpec((tm, HEAD_DIM), lambda i, s: (i % npb, 0)),
                  pl.BlockSpec((tm, HEAD_DIM), lambda i, s: (i % npb, 0))],
        out_specs=pl.BlockSpec((None, tm, A_WIDTH), lambda i, s: (s, i, 0)),
        out_shape=jax.ShapeDtypeStruct((2, m, A_WIDTH), F32),
        compiler_params=_cparams("parallel", "parallel"),
        name="qknorm_rope",
    )(proj, gains, cosf, sins)


def _subln(o, g, scale):
    ms = jnp.mean(o * o, axis=-1, keepdims=True)
    return o * lax.rsqrt(ms + LN_EPS) * g * scale


def _flash_kernel(lam_ref, q_ref, k_ref, v_ref, g_ref, o_ref, qs_ref, m_ref, l_ref, acc_ref, *, out_scale):
    qi = pl.program_id(2)
    ki = pl.program_id(3)
    tq = q_ref.shape[0]
    tk = k_ref.shape[0]

    @pl.when(ki == 0)
    def _():
        qs_ref[...] = (q_ref[...] * (HEAD_DIM ** -0.5 * LOG2E)).astype(BF16)
        m_ref[...] = jnp.full(m_ref.shape, NEG_BIG, F32)
        l_ref[...] = jnp.zeros(l_ref.shape, F32)
        acc_ref[...] = jnp.zeros(acc_ref.shape, F32)

    def update(masked):
        vb = v_ref[...].astype(BF16)
        for mp in range(2):
            sl = slice(mp * HEAD_DIM, (mp + 1) * HEAD_DIM)
            s = lax.dot_general(qs_ref[:, sl], k_ref[:, sl].astype(BF16), (((1,), (1,)), ((), ())),
                                preferred_element_type=F32)
            if masked:
                row = lax.broadcasted_iota(jnp.int32, (tq, tk), 0)
                col = lax.broadcasted_iota(jnp.int32, (tq, tk), 1)
                s = jnp.where(col <= row, s, NEG_BIG)
            m_prev = m_ref[mp]
            m_new = jnp.maximum(m_prev, jnp.max(s, axis=-1, keepdims=True))
            p = jnp.exp2(s - m_new)
            alpha = jnp.exp2(m_prev - m_new)
            l_ref[mp] = alpha * l_ref[mp] + jnp.sum(p, axis=-1, keepdims=True)
            acc_ref[mp] = alpha * acc_ref[mp] + jnp.dot(p.astype(BF16), vb, preferred_element_type=F32)
            m_ref[mp] = m_new

    @pl.when(ki < qi)
    def _():
        update(False)

    @pl.when(ki == qi)
    def _():
        update(True)
        lam = lam_ref[0]
        o = acc_ref[0] / l_ref[0] - lam * (acc_ref[1] / l_ref[1])
        o_ref[...] = _subln(o, g_ref[...], out_scale).astype(o_ref.dtype)


def flash_diff_attention(lam, qk, proj, v_col0, subln_g, batch, seq, out_scale):
    m = batch * seq
    tq = _pick(seq, (512, 256, 128))
    nq = seq // tq
    hw = 2 * HEAD_DIM
    vb0 = v_col0 // hw
    return pl.pallas_call(
        functools.partial(_flash_kernel, out_scale=out_scale),
        grid=(batch, N_HEADS_A, nq, nq),
        in_specs=[pl.BlockSpec(memory_space=pltpu.SMEM),
                  pl.BlockSpec((None, tq, hw), lambda b, h, qi, ki: (0, b * nq + qi, h)),
                  pl.BlockSpec((None, tq, hw), lambda b, h, qi, ki: (1, b * nq + jnp.minimum(ki, qi), h)),
                  pl.BlockSpec((tq, hw), lambda b, h, qi, ki: (b * nq + jnp.minimum(ki, qi), vb0 + h)),
                  pl.BlockSpec((1, hw), lambda b, h, qi, ki: (0, 0))],
        out_specs=pl.BlockSpec((tq, hw), lambda b, h, qi, ki: (b * nq + qi, h)),
        out_shape=jax.ShapeDtypeStruct((m, A_WIDTH), BF16),
        scratch_shapes=[pltpu.VMEM((tq, hw), BF16), pltpu.VMEM((2, tq, 1), F32), pltpu.VMEM((2, tq, 1), F32),
                        pltpu.VMEM((2, tq, hw), F32)],
        compiler_params=_cparams("parallel", "parallel", "parallel", "arbitrary"),
        name="flash_diff_attention",
    )(lam, qk, qk, proj, subln_g.reshape(1, hw))


def _decode_kernel(pt_ref, lam_ref, q_ref, kn_ref, vn_ref, *rest, n_steps, pps, out_scale):
    kc_refs = rest[:pps]
    vc_refs = rest[pps:2 * pps]
    g_ref, o_ref, m_ref, l_ref, acc_ref = rest[2 * pps:]
    p = pl.program_id(1)
    nh = N_HEADS_A
    rows = PAGE_SIZE * nh

    @pl.when(p == 0)
    def _():
        m_ref[...] = jnp.full(m_ref.shape, NEG_BIG, F32)
        l_ref[...] = jnp.zeros(l_ref.shape, F32)
        acc_ref[...] = jnp.zeros(acc_ref.shape, F32)

    q = q_ref[...] * (HEAD_DIM ** -0.5 * LOG2E)
    own_head = (lax.broadcasted_iota(jnp.int32, (nh, rows), 1) % nh
                == lax.broadcasted_iota(jnp.int32, (nh, rows), 0))
    qb = [q[mp * nh:(mp + 1) * nh].astype(BF16) for mp in range(2)]
    s_maps = []
    for mp in range(2):
        parts = []
        for kc_ref in kc_refs:
            k_m = kc_ref[pl.ds(mp, rows, stride=2), :].astype(BF16)
            s = lax.dot_general(qb[mp], k_m, (((1,), (1,)), ((), ())),
                                preferred_element_type=F32)
            parts.append(jnp.where(own_head, s, NEG_BIG))
        s_maps.append(jnp.concatenate(parts, axis=1))
    s = jnp.concatenate(s_maps, axis=0)
    m_prev = m_ref[...]
    m_new = jnp.maximum(m_prev, jnp.max(s, axis=-1, keepdims=True))
    pr = jnp.exp2(s - m_new)
    alpha = jnp.exp2(m_prev - m_new)
    l_ref[...] = alpha * l_ref[...] + jnp.sum(pr, axis=-1, keepdims=True)
    pr = pr.astype(BF16)
    pv = jnp.dot(pr[:, 0:rows], vc_refs[0][...].astype(BF16), preferred_element_type=F32)
    for j in range(1, pps):
        pv = pv + jnp.dot(pr[:, j * rows:(j + 1) * rows], vc_refs[j][...].astype(BF16),
                          preferred_element_type=F32)
    acc_ref[...] = alpha * acc_ref[...] + pv
    m_ref[...] = m_new

    @pl.when(p == n_steps - 1)
    def _():
        s_new = jnp.sum(q * kn_ref[...], axis=-1, keepdims=True)
        m_prev = m_ref[...]
        m_fin = jnp.maximum(m_prev, s_new)
        p_new = jnp.exp2(s_new - m_fin)
        alpha = jnp.exp2(m_prev - m_fin)
        l_fin = alpha * l_ref[...] + p_new
        vn = vn_ref[...]
        acc = alpha * acc_ref[...] + p_new * jnp.concatenate([vn, vn], axis=0)
        d = acc / l_fin
        o = d[0:nh] - lam_ref[0] * d[nh:2 * nh]
        o_ref[...] = _subln(o, g_ref[...], out_scale)


def paged_diff_attention(lam, qk, proj, v_col0, cache_k, cache_v, layer_idx, page_table, subln_g, out_scale):
    bs, n_pages = page_table.shape
    nh = N_HEADS_A
    hw = 2 * HEAD_DIM
    pps = _pick(n_pages, (DECODE_PAGES_PER_STEP, 4, 2, 1))
    n_steps = n_pages // pps
    qk_mh = qk.reshape(2, bs, nh, 2, HEAD_DIM).transpose(0, 1, 3, 2, 4).reshape(2, bs, 2 * nh, HEAD_DIM)
    vn = proj[:, v_col0:v_col0 + A_WIDTH].reshape(bs, nh, hw)
    ck = cache_k.reshape(cache_k.shape[0], cache_k.shape[1], PAGE_SIZE * nh * 2, HEAD_DIM)
    cv = cache_v.reshape(cache_v.shape[0], cache_v.shape[1], PAGE_SIZE * nh, hw)

    def page_spec(rows, width, j):
        return pl.BlockSpec((None, None, rows, width),
                            lambda b, p, pt: (layer_idx, pt[b * n_pages + p * pps + j], 0, 0))

    grid_spec = pltpu.PrefetchScalarGridSpec(
        num_scalar_prefetch=1,
        grid=(bs, n_steps),
        in_specs=[pl.BlockSpec(memory_space=pltpu.SMEM),
                  pl.BlockSpec((None, None, 2 * nh, HEAD_DIM), lambda b, p, pt: (0, b, 0, 0)),
                  pl.BlockSpec((None, None, 2 * nh, HEAD_DIM), lambda b, p, pt: (1, b, 0, 0)),
                  pl.BlockSpec((None, nh, hw), lambda b, p, pt: (b, 0, 0))]
                 + [page_spec(PAGE_SIZE * nh * 2, HEAD_DIM, j) for j in range(pps)]
                 + [page_spec(PAGE_SIZE * nh, hw, j) for j in range(pps)]
                 + [pl.BlockSpec((1, hw), lambda b, p, pt: (0, 0))],
        out_specs=pl.BlockSpec((None, nh, hw), lambda b, p, pt: (b, 0, 0)),
        scratch_shapes=[pltpu.VMEM((2 * nh, 1), F32), pltpu.VMEM((2 * nh, 1), F32),
                        pltpu.VMEM((2 * nh, hw), F32)],
    )
    out = pl.pallas_call(
        functools.partial(_decode_kernel, n_steps=n_steps, pps=pps, out_scale=out_scale),
        grid_spec=grid_spec,
        out_shape=jax.ShapeDtypeStruct((bs, nh, hw), F32),
        compiler_params=_cparams("parallel", "arbitrary"),
        name="paged_diff_attention",
    )(page_table.reshape(-1), lam, qk_mh, qk_mh, vn, *([ck] * pps), *([cv] * pps), subln_g.reshape(1, hw))
    return out.reshape(bs, A_WIDTH).astype(BF16)


def _rwkv_pad_cols(x, c_r):
    lead = x.shape[:-1]
    z = lambda n: jnp.zeros(lead + (n,), x.dtype)
    o = 3 * c_r
    return jnp.concatenate([x[..., :o],
                            x[..., o:o + DECAY_RANK], z(LANE - DECAY_RANK),
                            x[..., o + DECAY_RANK:o + DECAY_RANK + A_RANK], z(LANE - A_RANK),
                            x[..., o + DECAY_RANK + A_RANK:], z(LANE - G_RANK),
                            z(LANE)], axis=-1)


def _rwkv_unpad_cols(x, c_r):
    o = 3 * c_r
    return jnp.concatenate([x[..., :o], x[..., o:o + DECAY_RANK],
                            x[..., o + LANE:o + LANE + A_RANK],
                            x[..., o + 2 * LANE:o + 2 * LANE + G_RANK]], axis=-1)


def _rwkv_premix_kernel(pr_ref, prev_ref, mu_ref, w0_ref, w2_ref, a0_ref, a2_ref, g2_ref, kkp_ref, kap_ref,
                        o_ref, *, c_r, seq, single_step):
    i = pl.program_id(0)
    tm = pr_ref.shape[0]
    pr = pr_ref[...]
    if single_step:
        prev = prev_ref[...]
    else:
        rolled = pltpu.roll(pr, 1, 0)
        first = jnp.where((i * tm) % seq == 0, 0.0, 1.0) * prev_ref[7:8, :]
        row = lax.broadcasted_iota(jnp.int32, pr.shape, 0)
        prev = jnp.where(row == 0, first, rolled)
    ps = pr + mu_ref[...] * (prev - pr)
    r = ps[:, 0:c_r]
    kr = ps[:, c_r:2 * c_r]
    vr = ps[:, 2 * c_r:3 * c_r]
    wd = ps[:, 3 * c_r:3 * c_r + LANE]
    ad = ps[:, 3 * c_r + LANE:3 * c_r + 2 * LANE]
    gd = ps[:, 3 * c_r + 2 * LANE:3 * c_r + 3 * LANE]
    z = -(w0_ref[...] + jnp.dot(jnp.tanh(wd).astype(BF16), w2_ref[...], preferred_element_type=F32))
    softplus = jnp.maximum(z, 0.0) + jnp.log(1.0 + jnp.exp(-jnp.abs(z)))
    w_log = -softplus - 0.5
    a = _sigmoid(a0_ref[...] + jnp.dot(ad.astype(BF16), a2_ref[...], preferred_element_type=F32))
    g = jnp.dot(_sigmoid(gd).astype(BF16), g2_ref[...], preferred_element_type=F32)
    o_ref[0] = r
    o_ref[1] = kr * (1.0 + (a - 1.0) * kap_ref[...])
    o_ref[2] = vr
    o_ref[3] = -jnp.exp(w_log)
    o_ref[4] = kr * kkp_ref[...]
    o_ref[5] = a
    o_ref[6] = g


def rwkv_premix(proj, r_col0, prev, p, seq, single_step):
    m = proj.shape[0]
    c_r = p["w0"].shape[1]
    wr = 3 * c_r + 4 * LANE
    cb = r_col0 // wr
    tm = _pick(math.gcd(m, seq) if not single_step else m, (256, 128, 64, 32, 16, 8))
    if single_step:
        prev_spec = pl.BlockSpec((tm, wr), lambda i: (i, 0))
        prev_arg = prev
    else:
        prev_spec = pl.BlockSpec((8, wr), lambda i: (jnp.maximum(i * (tm // 8) - 1, 0), cb))
        prev_arg = proj
    vec = lambda n: pl.BlockSpec((1, n), lambda i: (0, 0))
    mat = lambda: pl.BlockSpec((LANE, c_r), lambda i: (0, 0))
    return pl.pallas_call(
        functools.partial(_rwkv_premix_kernel, c_r=c_r, seq=seq, single_step=single_step),
        grid=(m // tm,),
        in_specs=[pl.BlockSpec((tm, wr), lambda i: (i, cb)), prev_spec, vec(wr),
                  vec(c_r), mat(), vec(c_r), mat(), mat(), vec(c_r), vec(c_r)],
        out_specs=pl.BlockSpec((7, tm, c_r), lambda i: (0, i, 0)),
        out_shape=jax.ShapeDtypeStruct((7, m, c_r), F32),
        compiler_params=_cparams("parallel"),
        name="rwkv_premix",
    )(proj, prev_arg, p["mu"], p["w0"], p["w2"], p["a0"], p["a2"], p["g2"], p["kk"], p["ka"])


def _split3(x):
    hi = x.astype(BF16)
    r1 = x - hi.astype(F32)
    mid = r1.astype(BF16)
    lo = (r1 - mid.astype(F32)).astype(BF16)
    return hi, mid, lo


def _halfsum(x, m0):
    s0 = jnp.sum(jnp.where(m0, x, 0.0), axis=-1, keepdims=True)
    s1 = jnp.sum(jnp.where(m0, 0.0, x), axis=-1, keepdims=True)
    return jnp.where(m0, s0, s1)


def _bdot(a, b):
    return jnp.dot(a.astype(BF16), b.astype(BF16), preferred_element_type=F32)


def _bdot_nt(a, b):
    return lax.dot_general(a.astype(BF16), b.astype(BF16), (((1,), (1,)), ((), ())),
                           preferred_element_type=F32)


def _bdot_tn(a, b):
    return lax.dot_general(a.astype(BF16), b.astype(BF16), (((0,), (0,)), ((), ())),
                           preferred_element_type=F32)


def _same_head_mask():
    vrow = lax.broadcasted_iota(jnp.int32, (LANE, LANE), 0)
    kcol = lax.broadcasted_iota(jnp.int32, (LANE, LANE), 1)
    return (vrow < RWKV_HEAD) == (kcol < RWKV_HEAD)


def _rwkv_chunk_kernel(x_ref, rk_ref, rt2_ref, yvgb_ref, m2_ref, z_ref, ec_ref):
    C = x_ref.shape[1]
    N = RWKV_HEAD
    pairs = range(x_ref.shape[2] // LANE)
    sls = [slice(j * LANE, (j + 1) * LANE) for j in pairs]
    m0 = lax.broadcasted_iota(jnp.int32, (C, LANE), 1) < N
    m1 = jnp.logical_not(m0)
    trow = lax.broadcasted_iota(jnp.int32, (C, C), 0)
    tcol = lax.broadcasted_iota(jnp.int32, (C, C), 1)
    strict = tcol < trow
    incl = tcol <= trow
    tri = jnp.where(incl, 1.0, 0.0).astype(BF16)
    zeros = jnp.zeros((C, LANE), F32)
    same_head = _same_head_mask()

    lw = [x_ref[3, :, sl] for sl in sls]
    split = [_split3(x) for x in lw]
    cs = [jnp.dot(tri, s[0], preferred_element_type=F32) + jnp.dot(tri, s[1], preferred_element_type=F32)
          + jnp.dot(tri, s[2], preferred_element_type=F32) for s in split]

    r = [x_ref[0, :, sl] for sl in sls]
    k = [x_ref[1, :, sl] for sl in sls]
    v = [x_ref[2, :, sl] for sl in sls]
    a = [x_ref[5, :, sl] for sl in sls]
    kk = []
    for sl in sls:
        kr = x_ref[4, :, sl]
        kk.append(kr * lax.rsqrt(_halfsum(kr * kr, m0) + 1e-12))
    cs_end = [c[C - 1:C, :] for c in cs]
    al, rt, b_end, k_end, amat = [], [], [], [], []
    for j in pairs:
        e_neg = jnp.exp(-cs[j])
        al_j = -kk[j] * jnp.exp(cs[j] - lw[j])
        rt_j = r[j] * jnp.exp(cs[j])
        kb = kk[j] * a[j]
        e_end = jnp.exp(cs_end[j] - cs[j])
        al.append(al_j)
        rt.append(rt_j)
        b_end.append(kb * e_end)
        k_end.append(k[j] * e_end)
        amat.append(_bdot_nt(
            jnp.concatenate([jnp.where(m0, al_j, 0.0), jnp.where(m0, rt_j, 0.0),
                             jnp.where(m1, al_j, 0.0), jnp.where(m1, rt_j, 0.0)], axis=0),
            jnp.concatenate([kb * e_neg, k[j] * e_neg], axis=0)))

    heads = [(j, h) for j in pairs for h in range(2)]
    a_ab, a_rbk, x = {}, {}, {}
    for j, h in heads:
        base = 2 * C * h
        A = amat[j]
        a_ab[j, h] = jnp.where(strict, A[base:base + C, 0:C], 0.0)
        a_ak = jnp.where(strict, A[base:base + C, C:2 * C], 0.0)
        a_rbk[j, h] = jnp.concatenate([jnp.where(incl, A[base + C:base + 2 * C, 0:C], 0.0),
                                       jnp.where(incl, A[base + C:base + 2 * C, C:2 * C], 0.0)], axis=1)
        x[j, h] = jnp.concatenate([jnp.where(m0 if h == 0 else m1, al[j], 0.0), _bdot(a_ak, v[j])], axis=1)
    n = 1
    while n < C:
        for jh in heads:
            x[jh] = x[jh] + _bdot(a_ab[jh], x[jh])
        n *= 2
        if n < C:
            for jh in heads:
                a_ab[jh] = _bdot(a_ab[jh], a_ab[jh])

    uv = [jnp.where(m0, x[j, 0][:, LANE:], x[j, 1][:, LANE:]) for j in pairs]
    yh = {}
    for j, h in heads:
        yh[j, h] = _bdot(a_rbk[j, h],
                         jnp.concatenate([jnp.concatenate([x[j, h][:, :LANE], uv[j]], axis=1),
                                          jnp.concatenate([zeros, v[j]], axis=1)], axis=0))
    for j in pairs:
        sl = sls[j]
        al2 = x[j, 0][:, :LANE] + x[j, 1][:, :LANE]
        m2 = jnp.where(same_head, _bdot_tn(al2, b_end[j]), 0.0)
        z = jnp.where(same_head, _bdot_tn(jnp.concatenate([uv[j], v[j]], axis=0),
                                          jnp.concatenate([b_end[j], k_end[j]], axis=0)), 0.0)
        rt2 = rt[j] + yh[j, 0][:, :LANE] + yh[j, 1][:, :LANE]
        yv = jnp.where(m0, yh[j, 0][:, LANE:], yh[j, 1][:, LANE:])
        rt2_ref[:, sl] = rt2.astype(BF16)
        yvgb_ref[0, :, sl] = yv
        yvgb_ref[1, :, sl] = x_ref[6, :, sl]
        yvgb_ref[2, :, sl] = _halfsum(r[j] * k[j] * rk_ref[:, sl], m0) * v[j]
        m2_ref[j] = m2.astype(BF16)
        z_ref[j] = z
        ec_ref[:, sl] = jnp.exp(cs_end[j])


def _rwkv_state_kernel(rt2_ref, yvgb_ref, m2_ref, z_ref, ec_ref, lng_ref, lnb_ref, o_ref, s_out_ref, s_ref):
    c = pl.program_id(1)
    C = rt2_ref.shape[0]
    N = RWKV_HEAD

    @pl.when(c == 0)
    def _():
        s_ref[...] = jnp.zeros(s_ref.shape, F32)

    m0 = lax.broadcasted_iota(jnp.int32, (C, LANE), 1) < N
    for j in range(s_ref.shape[0]):
        sl = slice(j * LANE, (j + 1) * LANE)
        S = s_ref[j]
        Sb = S.astype(BF16)
        y = lax.dot_general(rt2_ref[:, sl], Sb, (((1,), (1,)), ((), ())),
                            preferred_element_type=F32) + yvgb_ref[0, :, sl]
        s_new = S * ec_ref[:, sl] + jnp.dot(Sb, m2_ref[j], preferred_element_type=F32) + z_ref[j]
        s_ref[j] = s_new
        s_out_ref[j] = s_new
        mu = _halfsum(y, m0) * (1.0 / N)
        d = y - mu
        var = _halfsum(d * d, m0) * (1.0 / N)
        yn = d * lax.rsqrt(var + LN_X_EPS) * lng_ref[:, sl] + lnb_ref[:, sl]
        o_ref[:, sl] = ((yn + yvgb_ref[2, :, sl]) * yvgb_ref[1, :, sl]).astype(o_ref.dtype)


def rwkv_scan(x7, rk, lnx_g, lnx_b, batch, seq):
    m, c_r = x7.shape[1], x7.shape[2]
    C = RWKV_CHUNK
    nc = seq // C
    nct = m // C
    npair = c_r // LANE
    pp = _pick(npair, (RWKV_PAIRS_PER_STEP, 2, 1))
    wl = pp * LANE
    rt2, yvgb, m2, z, ec = pl.pallas_call(
        _rwkv_chunk_kernel,
        grid=(nct, npair // pp),
        in_specs=[pl.BlockSpec((7, C, wl), lambda i, j: (0, i, j)),
                  pl.BlockSpec((1, wl), lambda i, j: (0, j))],
        out_specs=[pl.BlockSpec((C, wl), lambda i, j: (i, j)),
                   pl.BlockSpec((3, C, wl), lambda i, j: (0, i, j)),
                   pl.BlockSpec((None, pp, LANE, LANE), lambda i, j: (i, j, 0, 0)),
                   pl.BlockSpec((None, pp, LANE, LANE), lambda i, j: (i, j, 0, 0)),
                   pl.BlockSpec((None, 1, wl), lambda i, j: (i, 0, j))],
        out_shape=[jax.ShapeDtypeStruct((m, c_r), BF16),
                   jax.ShapeDtypeStruct((3, m, c_r), F32),
                   jax.ShapeDtypeStruct((nct, npair, LANE, LANE), BF16),
                   jax.ShapeDtypeStruct((nct, npair, LANE, LANE), F32),
                   jax.ShapeDtypeStruct((nct, 1, c_r), F32)],
        compiler_params=_cparams("parallel", "parallel"),
        name="rwkv_chunk",
    )(x7, rk)
    vec = lambda: pl.BlockSpec((1, c_r), lambda b, c: (0, 0))
    return pl.pallas_call(
        _rwkv_state_kernel,
        grid=(batch, nc),
        in_specs=[pl.BlockSpec((C, c_r), lambda b, c: (b * nc + c, 0)),
                  pl.BlockSpec((3, C, c_r), lambda b, c: (0, b * nc + c, 0)),
                  pl.BlockSpec((None, npair, LANE, LANE), lambda b, c: (b * nc + c, 0, 0, 0)),
                  pl.BlockSpec((None, npair, LANE, LANE), lambda b, c: (b * nc + c, 0, 0, 0)),
                  pl.BlockSpec((None, 1, c_r), lambda b, c: (b * nc + c, 0, 0)),
                  vec(), vec()],
        out_specs=[pl.BlockSpec((C, c_r), lambda b, c: (b * nc + c, 0)),
                   pl.BlockSpec((None, npair, LANE, LANE), lambda b, c: (b, 0, 0, 0))],
        out_shape=[jax.ShapeDtypeStruct((m, c_r), BF16),
                   jax.ShapeDtypeStruct((batch, npair, LANE, LANE), F32)],
        scratch_shapes=[pltpu.VMEM((npair, LANE, LANE), F32)],
        compiler_params=_cparams("parallel", "arbitrary"),
        name="rwkv_state",
    )(rt2, yvgb, m2, z, ec, lnx_g, lnx_b)


def _rwkv_step_kernel(s_ref, x_ref, rk_ref, lng_ref, lnb_ref, s_out_ref, o_ref):
    N = RWKV_HEAD
    S = s_ref[...]
    r = x_ref[0]
    k = x_ref[1]
    v = x_ref[2]
    w = jnp.exp(x_ref[3])
    kk = x_ref[4]
    a = x_ref[5]
    g = x_ref[6]
    kk = kk * lax.rsqrt(jnp.sum(kk * kk, axis=-1, keepdims=True) + 1e-12)
    eye = (lax.broadcasted_iota(jnp.int32, (1, N, N), 1) == lax.broadcasted_iota(jnp.int32, (1, N, N), 2))
    to_col = lambda row: jnp.sum(jnp.where(eye, row, 0.0), axis=-1, keepdims=True)
    to_row = lambda col: jnp.sum(jnp.where(eye, col, 0.0), axis=-2, keepdims=True)
    sk = -jnp.sum(S * kk, axis=-1, keepdims=True)
    s_new = S * w + sk * (kk * a) + to_col(v) * k
    s_out_ref[...] = s_new
    y = to_row(jnp.sum(s_new * r, axis=-1, keepdims=True))
    mu = jnp.mean(y, axis=-1, keepdims=True)
    d = y - mu
    var = jnp.mean(d * d, axis=-1, keepdims=True)
    yn = d * lax.rsqrt(var + LN_X_EPS) * lng_ref[...] + lnb_ref[...]
    bonus = jnp.sum(r * k * rk_ref[...], axis=-1, keepdims=True) * v
    o_ref[...] = (yn + bonus) * g


def rwkv_step(state, layer, x7, rk, lnx_g, lnx_b):
    _, bsz, nh, N, _ = state.shape
    bb = _pick(bsz, (16, 8, 4, 2, 1))
    x5 = x7.reshape(7, bsz, nh, 1, N)
    pv = lambda t: t.reshape(nh, 1, N)
    vec = lambda: pl.BlockSpec((None, 1, N), lambda i, h: (h, 0, 0))
    s_new, o = pl.pallas_call(
        _rwkv_step_kernel,
        grid=(bsz // bb, nh),
        in_specs=[pl.BlockSpec((None, bb, None, N, N), lambda i, h: (layer, i, h, 0, 0)),
                  pl.BlockSpec((7, bb, None, 1, N), lambda i, h: (0, i, h, 0, 0)),
                  vec(), vec(), vec()],
        out_specs=[pl.BlockSpec((bb, None, N, N), lambda i, h: (i, h, 0, 0)),
                   pl.BlockSpec((bb, None, 1, N), lambda i, h: (i, h, 0, 0))],
        out_shape=[jax.ShapeDtypeStruct(state.shape[1:], F32),
                   jax.ShapeDtypeStruct((bsz, nh, 1, N), F32)],
        compiler_params=_cparams("parallel", "parallel"),
        name="rwkv_step",
    )(state, x5, pv(rk), pv(lnx_g), pv(lnx_b))
    return s_new, o.reshape(bsz, nh * N)


def _ln_silu(y, g, b):
    mu = jnp.mean(y, axis=-1, keepdims=True)
    d = y - mu
    var = jnp.mean(d * d, axis=-1, keepdims=True)
    z = d * lax.rsqrt(var + LN_EPS) * g + b
    return z * _sigmoid(z)


def _conv_seq_kernel(cu_ref, halo_ref, w_ref, b_ref, lg_ref, lb_ref, o_ref, tail_ref, full_ref, *, seq):
    i = pl.program_id(0)
    tm = cu_ref.shape[0]
    c = cu_ref.shape[1] // 2
    glu = lambda ref: ref[:, 0:c] * _sigmoid(ref[:, c:2 * c])
    keep = jnp.where((i * tm) % seq == 0, 0.0, 1.0)
    full_ref[0:CONV_HALO, :] = glu(halo_ref) * keep
    full_ref[CONV_HALO:CONV_HALO + tm, :] = glu(cu_ref)
    acc = jnp.zeros((tm, c), F32) + b_ref[...]
    off = CONV_HALO - (CONV_W - 1)
    for j in range(CONV_W):
        acc = acc + w_ref[j:j + 1, :] * full_ref[off + j:off + j + tm, :]
    o_ref[...] = _ln_silu(acc, lg_ref[...], lb_ref[...]).astype(o_ref.dtype)
    tail_ref[...] = full_ref[tm:tm + CONV_HALO, :]


def conv_seq(proj, c_col0, conv_w, conv_b, ln_g, ln_b, batch, seq):
    m = proj.shape[0]
    c = conv_w.shape[1]
    tm = _pick(seq, (256, 128, 64, 32))
    cb = c_col0 // (2 * c)
    hb = tm // CONV_HALO
    vec = lambda: pl.BlockSpec((1, c), lambda i: (0, 0))
    return pl.pallas_call(
        functools.partial(_conv_seq_kernel, seq=seq),
        grid=(m // tm,),
        in_specs=[pl.BlockSpec((tm, 2 * c), lambda i: (i, cb)),
                  pl.BlockSpec((CONV_HALO, 2 * c), lambda i: (jnp.maximum(i * hb - 1, 0), cb)),
                  pl.BlockSpec((CONV_W, c), lambda i: (0, 0)), vec(), vec(), vec()],
        out_specs=[pl.BlockSpec((tm, c), lambda i: (i, 0)),
                   pl.BlockSpec((None, CONV_HALO, c), lambda i: ((i * tm) // seq, 0, 0))],
        out_shape=[jax.ShapeDtypeStruct((m, c), BF16),
                   jax.ShapeDtypeStruct((batch, CONV_HALO, c), F32)],
        scratch_shapes=[pltpu.VMEM((CONV_HALO + tm, c), F32)],
        compiler_params=_cparams("arbitrary"),
        name="conv_seq",
    )(proj, proj, conv_w, conv_b.reshape(1, c), ln_g.reshape(1, c), ln_b.reshape(1, c))


def _conv_step_kernel(cu_ref, cp_ref, w_ref, b_ref, lg_ref, lb_ref, o_ref, cn_ref):
    c = cu_ref.shape[1] // 2
    u = cu_ref[:, 0:c] * _sigmoid(cu_ref[:, c:2 * c])
    acc = b_ref[...] + w_ref[CONV_W - 1:CONV_W, :] * u
    for j in range(CONV_W - 1):
        acc = acc + w_ref[j:j + 1, :] * cp_ref[:, j, :]
    o_ref[...] = _ln_silu(acc, lg_ref[...], lb_ref[...]).astype(o_ref.dtype)
    cn_ref[:, 0:CONV_W - 2, :] = cp_ref[:, 1:CONV_W - 1, :]
    cn_ref[:, CONV_W - 2, :] = u


def conv_step(proj, c_col0, conv_prev, layer, conv_w, conv_b, ln_g, ln_b):
    bsz = proj.shape[0]
    c = conv_w.shape[1]
    bb = _pick(bsz, (8,))
    cb = c_col0 // (2 * c)
    vec = lambda: pl.BlockSpec((1, c), lambda i: (0, 0))
    return pl.pallas_call(
        _conv_step_kernel,
        grid=(bsz // bb,),
        in_specs=[pl.BlockSpec((bb, 2 * c), lambda i: (i, cb)),
                  pl.BlockSpec((None, bb, CONV_W - 1, c), lambda i: (layer, i, 0, 0)),
                  pl.BlockSpec((CONV_W, c), lambda i: (0, 0)), vec(), vec(), vec()],
        out_specs=[pl.BlockSpec((bb, c), lambda i: (i, 0)),
                   pl.BlockSpec((bb, CONV_W - 1, c), lambda i: (i, 0, 0))],
        out_shape=[jax.ShapeDtypeStruct((bsz, c), BF16),
                   jax.ShapeDtypeStruct((bsz, CONV_W - 1, c), F32)],
        compiler_params=_cparams("parallel"),
        name="conv_step",
    )(proj, conv_prev, conv_w, conv_b.reshape(1, c), ln_g.reshape(1, c), ln_b.reshape(1, c))


def _merge_kernel(xa_ref, xb_ref, xc_ref, wa_ref, wb_ref, wc_ref, ga_ref, gb_ref, gc_ref, o_ref):
    m = _sigmoid(ga_ref[...]) * jnp.dot(xa_ref[...], wa_ref[...], preferred_element_type=F32)
    m = m + _sigmoid(gb_ref[...]) * jnp.dot(xb_ref[...], wb_ref[...], preferred_element_type=F32)
    m = m + _sigmoid(gc_ref[...]) * jnp.dot(xc_ref[...], wc_ref[...], preferred_element_type=F32)
    o_ref[...] = m.astype(o_ref.dtype)


def gated_merge(o_a, o_b, o_c, w_pa, w_pb, w_pc, layer, proj, g_col0):
    m = o_a.shape[0]
    d = w_pa.shape[-1]
    tm = _pick(m, (512, 256, 128, 64, 32, 16))
    tn = _pick(math.gcd(d, g_col0), (512, 256, 128))
    gb0 = g_col0 // tn
    nb = d // tn
    xs = lambda kd: pl.BlockSpec((tm, kd), lambda i, j: (i, 0))
    ws = lambda kd: pl.BlockSpec((None, kd, tn), lambda i, j: (layer, 0, j))
    gs = lambda br: pl.BlockSpec((tm, tn), lambda i, j: (i, gb0 + br * nb + j))
    return pl.pallas_call(
        _merge_kernel,
        grid=(m // tm, nb),
        in_specs=[xs(o_a.shape[1]), xs(o_b.shape[1]), xs(o_c.shape[1]),
                  ws(w_pa.shape[1]), ws(w_pb.shape[1]), ws(w_pc.shape[1]), gs(0), gs(1), gs(2)],
        out_specs=pl.BlockSpec((tm, tn), lambda i, j: (i, j)),
        out_shape=jax.ShapeDtypeStruct((m, d), BF16),
        compiler_params=_cparams("parallel", "parallel"),
        name="gated_merge",
    )(o_a, o_b, o_c, w_pa, w_pb, w_pc, proj, proj, proj)


def _rope_tables(pos):
    half = HEAD_DIM // 2
    freqs = ROPE_THETA ** (-jnp.arange(half, dtype=F32) * (2.0 / HEAD_DIM))
    ang = pos.astype(F32)[:, None] * freqs[None, :]
    cos, sin = jnp.cos(ang), jnp.sin(ang)
    return jnp.concatenate([cos, cos], axis=-1), jnp.concatenate([-sin, sin], axis=-1)


def _layer(x2d, batch, seq, lp, lam, lambda_init, tables, sample_state):
    d_model = x2d.shape[1]
    layer = lp["layer"]
    c_r = lp["w_pb"].shape[1]
    c_c = lp["w_pc"].shape[1]
    col_v = 2 * A_WIDTH
    col_g = 3 * A_WIDTH
    col_c = col_g + 3 * d_model
    col_r = col_c + 2 * c_c
    out_scale = 1.0 - lambda_init

    h = rmsnorm_bf16(x2d, lp["norm1_g"])
    proj = matmul(h, lp["w_in_cat"], F32)

    qk = qknorm_rope(proj, lp["qk_gain"], *tables)
    if sample_state is None:
        o_a = flash_diff_attention(lam, qk, proj, col_v, lp["subln_g"], batch, seq, out_scale)
    else:
        o_a = paged_diff_attention(lam, qk, proj, col_v, sample_state["cache_k"], sample_state["cache_v"],
                                   sample_state["layer"], sample_state["page_table"], lp["subln_g"], out_scale)
    new_k = qk[1]
    new_v = proj[:, col_v:col_v + A_WIDTH]

    last_rows = proj[seq - 1::seq, col_r:]
    if sample_state is None:
        x7 = rwkv_premix(proj, col_r, None, lp["rwkv"], seq, False)
        o_b, pair_state = rwkv_scan(x7, lp["rk"], lp["lnx_g"], lp["lnx_b"], batch, seq)
        nh = c_r // RWKV_HEAD
        ps = pair_state.reshape(batch, nh // 2, 2, RWKV_HEAD, 2, RWKV_HEAD)
        new_state = jnp.stack([ps[:, :, 0, :, 0, :], ps[:, :, 1, :, 1, :]], axis=2).reshape(
            batch, nh, RWKV_HEAD, RWKV_HEAD)
    else:
        x7 = rwkv_premix(proj, col_r, _rwkv_pad_cols(sample_state["shift"], c_r), lp["rwkv"], seq, True)
        new_state, o_b = rwkv_step(sample_state["rwkv"], layer, x7, lp["rk"], lp["lnx_g"], lp["lnx_b"])
        o_b = o_b.astype(BF16)
    new_shift = _rwkv_unpad_cols(last_rows, c_r)

    if sample_state is None:
        o_c, tail = conv_seq(proj, col_c, lp["conv_w"], lp["conv_b"], lp["conv_ln_g"], lp["conv_ln_b"], batch, seq)
        new_conv = tail[:, CONV_HALO - (CONV_W - 1):]
    else:
        o_c, new_conv = conv_step(proj, col_c, sample_state["conv"], layer, lp["conv_w"], lp["conv_b"],
                                  lp["conv_ln_g"], lp["conv_ln_b"])

    merged = gated_merge(o_a, o_b, o_c, lp["w_pa"], lp["w_pb"], lp["w_pc"], layer, proj, col_g)
    x2d = matmul(merged, lp["w_o"], F32, "residual", x2d, layer=layer)
    h2 = rmsnorm_bf16(x2d, lp["norm2_g"])
    ff = matmul(h2, lp["w_mlp1"], BF16, "relu2", layer=layer)
    x2d = matmul(ff, lp["w_mlp2"], F32, "residual", x2d, layer=layer)
    return x2d, new_k, new_v, new_state, new_shift, new_conv


def kernel(x_prompt, x_sample, cache_k, cache_v, page_table, state_rwkv, state_shift, state_conv, norm1_g, w_in, q_norm_g, k_norm_g, lambda_q1, lambda_k1, lambda_q2, lambda_k2, subln_g, rwkv_mu, rwkv_w0, rwkv_w2, rwkv_a0, rwkv_a2, rwkv_g2, rwkv_kk, rwkv_ka, rwkv_rk, rwkv_lnx_g, rwkv_lnx_b, conv_w, conv_b, conv_ln_g, conv_ln_b, w_pa, w_pb, w_pc, w_o, norm2_g, w_mlp1, w_mlp2):
    depth = w_in.shape[0]
    b_p, t_p, d_model = x_prompt.shape
    b_s, t_s, _ = x_sample.shape
    assert t_s == 1
    c_r = w_pb.shape[1]
    c_c = w_pc.shape[1]
    past_len = page_table.shape[1] * PAGE_SIZE
    o_r = 3 * A_WIDTH
    o_c = o_r + 3 * c_r + DECAY_RANK + A_RANK + G_RANK
    o_g = o_c + 2 * c_c

    tab_p = _rope_tables(jnp.arange(t_p))
    tab_s = _rope_tables(jnp.full((b_s,), past_len))

    pad_rows = lambda w, n: jnp.concatenate([w, jnp.zeros((n - w.shape[0], w.shape[1]), w.dtype)], axis=0)
    row = lambda v: v.reshape(1, -1)

    stacked = dict(w_pa=w_pa.astype(BF16), w_pb=w_pb.astype(BF16), w_pc=w_pc.astype(BF16),
                   w_o=w_o.astype(BF16), w_mlp1=w_mlp1.astype(BF16), w_mlp2=w_mlp2.astype(BF16))

    yp = x_prompt.reshape(b_p * t_p, d_model)
    ys = x_sample.reshape(b_s, d_model)
    outs = [[] for _ in range(10)]
    for l in range(depth):
        wl = w_in[l]
        lp = dict(
            norm1_g=norm1_g[l], norm2_g=norm2_g[l],
            w_in_cat=jnp.concatenate([wl[:, :o_r], wl[:, o_g:], wl[:, o_c:o_g],
                                      _rwkv_pad_cols(wl[:, o_r:o_c], c_r)], axis=1).astype(BF16),
            qk_gain=jnp.stack([q_norm_g[l], k_norm_g[l]]).reshape(2, 1, HEAD_DIM),
            subln_g=subln_g[l],
            rwkv=dict(mu=row(_rwkv_pad_cols(rwkv_mu[l], c_r)), w0=row(rwkv_w0[l]),
                      w2=pad_rows(rwkv_w2[l], LANE).astype(BF16), a0=row(rwkv_a0[l]),
                      a2=pad_rows(rwkv_a2[l], LANE).astype(BF16), g2=pad_rows(rwkv_g2[l], LANE).astype(BF16),
                      kk=row(rwkv_kk[l]), ka=row(rwkv_ka[l])),
            rk=row(rwkv_rk[l]), lnx_g=row(rwkv_lnx_g[l]), lnx_b=row(rwkv_lnx_b[l]),
            conv_w=conv_w[l], conv_b=conv_b[l], conv_ln_g=conv_ln_g[l], conv_ln_b=conv_ln_b[l],
            w_pa=stacked["w_pa"], w_pb=stacked["w_pb"], w_pc=stacked["w_pc"],
            w_o=stacked["w_o"], w_mlp1=stacked["w_mlp1"], w_mlp2=stacked["w_mlp2"], layer=l,
        )
        lambda_init = 0.8 - 0.6 * math.exp(-0.3 * l)
        lam = (jnp.exp(jnp.sum(lambda_q1[l] * lambda_k1[l])) - jnp.exp(jnp.sum(lambda_q2[l] * lambda_k2[l]))
               + lambda_init).reshape(1).astype(F32)

        yp, k_new, v_new, s_new, sh_new, c_new = _layer(yp, b_p, t_p, lp, lam, lambda_init, tab_p, None)
        outs[0].append(k_new.reshape(b_p, t_p, N_HEADS_A, 2, HEAD_DIM))
        outs[1].append(v_new.reshape(b_p, t_p, N_HEADS_A, 2 * HEAD_DIM))
        outs[4].append(s_new)
        outs[6].append(sh_new)
        outs[8].append(c_new)

        st = dict(cache_k=cache_k, cache_v=cache_v, layer=l, page_table=page_table, rwkv=state_rwkv,
                  shift=state_shift[l], conv=state_conv)
        ys, k_new, v_new, s_new, sh_new, c_new = _layer(ys, b_s, 1, lp, lam, lambda_init, tab_s, st)
        outs[2].append(k_new.reshape(b_s, 1, N_HEADS_A, 2, HEAD_DIM))
        outs[3].append(v_new.reshape(b_s, 1, N_HEADS_A, 2 * HEAD_DIM))
        outs[5].append(s_new)
        outs[7].append(sh_new)
        outs[9].append(c_new)
    return (yp.reshape(b_p, t_p, d_model), ys.reshape(b_s, 1, d_model)) + tuple(jnp.stack(o) for o in outs)
```

```python
import functools
import math

import jax
import jax.numpy as jnp
from jax import lax
from jax.experimental import pallas as pl
from jax.experimental.pallas import tpu as pltpu

F32 = jnp.float32
BF16 = jnp.bfloat16

N_HEADS_A = 8
HEAD_DIM = 128
A_WIDTH = N_HEADS_A * 2 * HEAD_DIM
ROPE_THETA = 10000.0
RWKV_HEAD = 64
DECAY_RANK = 64
A_RANK = 64
G_RANK = 32
CONV_W = 31
PAGE_SIZE = 128
RMS_EPS = 1e-6
LN_EPS = 1e-5
LN_X_EPS = 64e-5

LANE = 128
RWKV_CHUNK = 64
RWKV_PAIRS_PER_STEP = 8
WIDE_TN = 2560
DECODE_PAGES_PER_STEP = 8
CONV_HALO = 32
NEG_BIG = -1e30
LOG2E = 1.4426950408889634


def _sigmoid(x):
    return 1.0 / (1.0 + jnp.exp(-x))


def _cparams(*sem):
    return pltpu.CompilerParams(dimension_semantics=sem, vmem_limit_bytes=56 * 1024 * 1024)


def _pick(n, cands):
    for c in cands:
        if n % c == 0:
            return c
    raise ValueError(f"no tile for {n}")


def _rmsnorm_kernel(x_ref, g_ref, o_ref):
    x = x_ref[...]
    ms = jnp.mean(x * x, axis=-1, keepdims=True)
    o_ref[...] = (x * lax.rsqrt(ms + RMS_EPS) * g_ref[...]).astype(o_ref.dtype)


def rmsnorm_bf16(x, g):
    m, d = x.shape
    tm = _pick(m, (512, 256, 128, 64, 32, 16))
    return pl.pallas_call(
        _rmsnorm_kernel,
        grid=(m // tm,),
        in_specs=[pl.BlockSpec((tm, d), lambda i: (i, 0)), pl.BlockSpec((1, d), lambda i: (0, 0))],
        out_specs=pl.BlockSpec((tm, d), lambda i: (i, 0)),
        out_shape=jax.ShapeDtypeStruct((m, d), BF16),
        compiler_params=_cparams("parallel"),
        name="rmsnorm",
    )(x, g.reshape(1, d))


def _matmul_kernel(*refs, nk, epilogue):
    if epilogue == "residual":
        x_ref, w_ref, res_ref, o_ref = refs[:4]
        rest = refs[4:]
    else:
        x_ref, w_ref, o_ref = refs[:3]
        res_ref = None
        rest = refs[3:]

    def finish(acc):
        if epilogue == "relu2":
            r = jnp.maximum(acc, 0.0)
            acc = r * r
        elif epilogue == "residual":
            acc = acc + res_ref[...]
        o_ref[...] = acc.astype(o_ref.dtype)

    part = jnp.dot(x_ref[...], w_ref[...], preferred_element_type=F32)
    if nk == 1:
        finish(part)
    else:
        acc_ref = rest[0]
        k = pl.program_id(2)

        @pl.when(k == 0)
        def _():
            acc_ref[...] = part

        @pl.when(k > 0)
        def _():
            acc_ref[...] += part

        @pl.when(k == nk - 1)
        def _():
            finish(acc_ref[...])


def matmul(x, w, out_dtype, epilogue=None, res=None, layer=None):
    m, kdim = x.shape
    n = w.shape[-1]
    tm = _pick(m, (1024, 512, 256, 128, 64, 32, 16))
    tn = _pick(n, (1024, 512, 256, 128))
    tk = _pick(kdim, (2048, 1024, 512))
    if n % WIDE_TN == 0 and n % 1024 != 0 and m % 512 == 0:
        tm, tn = 512, WIDE_TN
    nk = kdim // tk
    if layer is None:
        w_spec = pl.BlockSpec((tk, tn), lambda i, j, k: (k, j))
    else:
        w_spec = pl.BlockSpec((None, tk, tn), lambda i, j, k: (layer, k, j))
    in_specs = [pl.BlockSpec((tm, tk), lambda i, j, k: (i, k)), w_spec]
    args = [x, w]
    if epilogue == "residual":
        in_specs.append(pl.BlockSpec((tm, tn), lambda i, j, k: (i, j)))
        args.append(res)
    scratch = [pltpu.VMEM((tm, tn), F32)] if nk > 1 else []
    return pl.pallas_call(
        functools.partial(_matmul_kernel, nk=nk, epilogue=epilogue),
        grid=(m // tm, n // tn, nk),
        in_specs=in_specs,
        out_specs=pl.BlockSpec((tm, tn), lambda i, j, k: (i, j)),
        out_shape=jax.ShapeDtypeStruct((m, n), out_dtype),
        scratch_shapes=scratch,
        compiler_params=_cparams("parallel", "parallel", "arbitrary"),
        name="matmul_" + (epilogue or "plain"),
    )(*args)


def _qknorm_rope_kernel(x_ref, g_ref, cos_ref, sin_ref, o_ref):
    g = g_ref[...]
    cosf = cos_ref[...]
    sins = sin_ref[...]
    for grp in range(x_ref.shape[1] // HEAD_DIM):
        sl = slice(grp * HEAD_DIM, (grp + 1) * HEAD_DIM)
        x = x_ref[:, sl]
        ms = jnp.mean(x * x, axis=-1, keepdims=True)
        y = x * lax.rsqrt(ms + RMS_EPS) * g
        o_ref[:, sl] = y * cosf + pltpu.roll(y, HEAD_DIM // 2, 1) * sins


def qknorm_rope(proj, gains, cosf, sins):
    m = proj.shape[0]
    p = cosf.shape[0]
    tm = _pick(math.gcd(m, p), (256, 128, 64, 32, 16, 8))
    npb = p // tm
    return pl.pallas_call(
        _qknorm_rope_kernel,
        grid=(m // tm, 2),
        in_specs=[pl.BlockSpec((tm, A_WIDTH), lambda i, s: (i, s)),
                  pl.BlockSpec((None, 1, HEAD_DIM), lambda i, s: (s, 0, 0)),
                  pl.BlockSpec((tm, HEAD_DIM), lambda i, s: (i % npb, 0)),
                  pl.BlockSpec((tm, HEAD_DIM), lambda i, s: (i % npb, 0))],
        out_specs=pl.BlockSpec((None, tm, A_WIDTH), lambda i, s: (s, i, 0)),
        out_shape=jax.ShapeDtypeStruct((2, m, A_WIDTH), F32),
        compiler_params=_cparams("parallel", "parallel"),
        name="qknorm_rope",
    )(proj, gains, cosf, sins)


def _subln(o, g, scale):
    ms = jnp.mean(o * o, axis=-1, keepdims=True)
    return o * lax.rsqrt(ms + LN_EPS) * g * scale


def _flash_kernel(lam_ref, q_ref, k_ref, v_ref, g_ref, o_ref, qs_ref, m_ref, l_ref, acc_ref, *, out_scale):
    qi = pl.program_id(2)
    ki = pl.program_id(3)
    tq = q_ref.shape[0]
    tk = k_ref.shape[0]

    @pl.when(ki == 0)
    def _():
        qs_ref[...] = (q_ref[...] * (HEAD_DIM ** -0.5 * LOG2E)).astype(BF16)
        m_ref[...] = jnp.full(m_ref.shape, NEG_BIG, F32)
        l_ref[...] = jnp.zeros(l_ref.shape, F32)
        acc_ref[...] = jnp.zeros(acc_ref.shape, F32)

    def update(masked):
        vb = v_ref[...].astype(BF16)
        for mp in range(2):
            sl = slice(mp * HEAD_DIM, (mp + 1) * HEAD_DIM)
            s = lax.dot_general(qs_ref[:, sl], k_ref[:, sl].astype(BF16), (((1,), (1,)), ((), ())),
                                preferred_element_type=F32)
            if masked:
                row = lax.broadcasted_iota(jnp.int32, (tq, tk), 0)
                col = lax.broadcasted_iota(jnp.int32, (tq, tk), 1)
                s = jnp.where(col <= row, s, NEG_BIG)
            m_prev = m_ref[mp]
            m_new = jnp.maximum(m_prev, jnp.max(s, axis=-1, keepdims=True))
            p = jnp.exp2(s - m_new)
            alpha = jnp.exp2(m_prev - m_new)
            l_ref[mp] = alpha * l_ref[mp] + jnp.sum(p, axis=-1, keepdims=True)
            acc_ref[mp] = alpha * acc_ref[mp] + jnp.dot(p.astype(BF16), vb, preferred_element_type=F32)
            m_ref[mp] = m_new

    @pl.when(ki < qi)
    def _():
        update(False)

    @pl.when(ki == qi)
    def _():
        update(True)
        lam = lam_ref[0]
        o = acc_ref[0] / l_ref[0] - lam * (acc_ref[1] / l_ref[1])
        o_ref[...] = _subln(o, g_ref[...], out_scale).astype(o_ref.dtype)


def flash_diff_attention(lam, qk, proj, v_col0, subln_g, batch, seq, out_scale):
    m = batch * seq
    tq = _pick(seq, (512, 256, 128))
    nq = seq // tq
    hw = 2 * HEAD_DIM
    vb0 = v_col0 // hw
    return pl.pallas_call(
        functools.partial(_flash_kernel, out_scale=out_scale),
        grid=(batch, N_HEADS_A, nq, nq),
        in_specs=[pl.BlockSpec(memory_space=pltpu.SMEM),
                  pl.BlockSpec((None, tq, hw), lambda b, h, qi, ki: (0, b * nq + qi, h)),
                  pl.BlockSpec((None, tq, hw), lambda b, h, qi, ki: (1, b * nq + jnp.minimum(ki, qi), h)),
                  pl.BlockSpec((tq, hw), lambda b, h, qi, ki: (b * nq + jnp.minimum(ki, qi), vb0 + h)),
                  pl.BlockSpec((1, hw), lambda b, h, qi, ki: (0, 0))],
        out_specs=pl.BlockSpec((tq, hw), lambda b, h, qi, ki: (b * nq + qi, h)),
        out_shape=jax.ShapeDtypeStruct((m, A_WIDTH), BF16),
        scratch_shapes=[pltpu.VMEM((tq, hw), BF16), pltpu.VMEM((2, tq, 1), F32), pltpu.VMEM((2, tq, 1), F32),
                        pltpu.VMEM((2, tq, hw), F32)],
        compiler_params=_cparams("parallel", "parallel", "parallel", "arbitrary"),
        name="flash_diff_attention",
    )(lam, qk, qk, proj, subln_g.reshape(1, hw))


def _decode_kernel(pt_ref, lam_ref, q_ref, kn_ref, vn_ref, *rest, n_steps, pps, out_scale):
    kc_refs = rest[:pps]
    vc_refs = rest[pps:2 * pps]
    g_ref, o_ref, m_ref, l_ref, acc_ref = rest[2 * pps:]
    p = pl.program_id(1)
    nh = N_HEADS_A
    rows = PAGE_SIZE * nh

    @pl.when(p == 0)
    def _():
        m_ref[...] = jnp.full(m_ref.shape, NEG_BIG, F32)
        l_ref[...] = jnp.zeros(l_ref.shape, F32)
        acc_ref[...] = jnp.zeros(acc_ref.shape, F32)

    q = q_ref[...] * (HEAD_DIM ** -0.5 * LOG2E)
    own_head = (lax.broadcasted_iota(jnp.int32, (nh, rows), 1) % nh
                == lax.broadcasted_iota(jnp.int32, (nh, rows), 0))
    qb = [q[mp * nh:(mp + 1) * nh].astype(BF16) for mp in range(2)]
    s_maps = []
    for mp in range(2):
        parts = []
        for kc_ref in kc_refs:
            k_m = kc_ref[pl.ds(mp, rows, stride=2), :].astype(BF16)
            s = lax.dot_general(qb[mp], k_m, (((1,), (1,)), ((), ())),
                                preferred_element_type=F32)
            parts.append(jnp.where(own_head, s, NEG_BIG))
        s_maps.append(jnp.concatenate(parts, axis=1))
    s = jnp.concatenate(s_maps, axis=0)
    m_prev = m_ref[...]
    m_new = jnp.maximum(m_prev, jnp.max(s, axis=-1, keepdims=True))
    pr = jnp.exp2(s - m_new)
    alpha = jnp.exp2(m_prev - m_new)
    l_ref[...] = alpha * l_ref[...] + jnp.sum(pr, axis=-1, keepdims=True)
    pr = pr.astype(BF16)
    pv = jnp.dot(pr[:, 0:rows], vc_refs[0][...].astype(BF16), preferred_element_type=F32)
    for j in range(1, pps):
        pv = pv + jnp.dot(pr[:, j * rows:(j + 1) * rows], vc_refs[j][...].astype(BF16),
                          preferred_element_type=F32)
    acc_ref[...] = alpha * acc_ref[...] + pv
    m_ref[...] = m_new

    @pl.when(p == n_steps - 1)
    def _():
        s_new = jnp.sum(q * kn_ref[...], axis=-1, keepdims=True)
        m_prev = m_ref[...]
        m_fin = jnp.maximum(m_prev, s_new)
        p_new = jnp.exp2(s_new - m_fin)
        alpha = jnp.exp2(m_prev - m_fin)
        l_fin = alpha * l_ref[...] + p_new
        vn = vn_ref[...]
        acc = alpha * acc_ref[...] + p_new * jnp.concatenate([vn, vn], axis=0)
        d = acc / l_fin
        o = d[0:nh] - lam_ref[0] * d[nh:2 * nh]
        o_ref[...] = _subln(o, g_ref[...], out_scale)


def paged_diff_attention(lam, qk, proj, v_col0, cache_k, cache_v, layer_idx, page_table, subln_g, out_scale):
    bs, n_pages = page_table.shape
    nh = N_HEADS_A
    hw = 2 * HEAD_DIM
    pps = _pick(n_pages, (DECODE_PAGES_PER_STEP, 4, 2, 1))
    n_steps = n_pages // pps
    qk_mh = qk.reshape(2, bs, nh, 2, HEAD_DIM).transpose(0, 1, 3, 2, 4).reshape(2, bs, 2 * nh, HEAD_DIM)
    vn = proj[:, v_col0:v_col0 + A_WIDTH].reshape(bs, nh, hw)
    ck = cache_k.reshape(cache_k.shape[0], cache_k.shape[1], PAGE_SIZE * nh * 2, HEAD_DIM)
    cv = cache_v.reshape(cache_v.shape[0], cache_v.shape[1], PAGE_SIZE * nh, hw)

    def page_spec(rows, width, j):
        return pl.BlockSpec((None, None, rows, width),
                            lambda b, p, pt: (layer_idx, pt[b * n_pages + p * pps + j], 0, 0))

    grid_spec = pltpu.PrefetchScalarGridSpec(
        num_scalar_prefetch=1,
        grid=(bs, n_steps),
        in_specs=[pl.BlockSpec(memory_space=pltpu.SMEM),
                  pl.BlockSpec((None, None, 2 * nh, HEAD_DIM), lambda b, p, pt: (0, b, 0, 0)),
                  pl.BlockSpec((None, None, 2 * nh, HEAD_DIM), lambda b, p, pt: (1, b, 0, 0)),
                  pl.BlockSpec((None, nh, hw), lambda b, p, pt: (b, 0, 0))]
                 + [page_spec(PAGE_SIZE * nh * 2, HEAD_DIM, j) for j in range(pps)]
                 + [page_spec(PAGE_SIZE * nh, hw, j) for j in range(pps)]
                 + [pl.BlockSpec((1, hw), lambda b, p, pt: (0, 0))],
        out_specs=pl.BlockSpec((None, nh, hw), lambda b, p, pt: (b, 0, 0)),
        scratch_shapes=[pltpu.VMEM((2 * nh, 1), F32), pltpu.VMEM((2 * nh, 1), F32),
                        pltpu.VMEM((2 * nh, hw), F32)],
    )
    out = pl.pallas_call(
        functools.partial(_decode_kernel, n_steps=n_steps, pps=pps, out_scale=out_scale),
        grid_spec=grid_spec,
        out_shape=jax.ShapeDtypeStruct((bs, nh, hw), F32),
        compiler_params=_cparams("parallel", "arbitrary"),
        name="paged_diff_attention",
    )(page_table.reshape(-1), lam, qk_mh, qk_mh, vn, *([ck] * pps), *([cv] * pps), subln_g.reshape(1, hw))
    return out.reshape(bs, A_WIDTH).astype(BF16)


def _rwkv_pad_cols(x, c_r):
    lead = x.shape[:-1]
    z = lambda n: jnp.zeros(lead + (n,), x.dtype)
    o = 3 * c_r
    return jnp.concatenate([x[..., :o],
                            x[..., o:o + DECAY_RANK], z(LANE - DECAY_RANK),
                            x[..., o + DECAY_RANK:o + DECAY_RANK + A_RANK], z(LANE - A_RANK),
                            x[..., o + DECAY_RANK + A_RANK:], z(LANE - G_RANK),
                            z(LANE)], axis=-1)


def _rwkv_unpad_cols(x, c_r):
    o = 3 * c_r
    return jnp.concatenate([x[..., :o], x[..., o:o + DECAY_RANK],
                            x[..., o + LANE:o + LANE + A_RANK],
                            x[..., o + 2 * LANE:o + 2 * LANE + G_RANK]], axis=-1)


def _rwkv_premix_kernel(pr_ref, prev_ref, mu_ref, w0_ref, w2_ref, a0_ref, a2_ref, g2_ref, kkp_ref, kap_ref,
                        o_ref, *, c_r, seq, single_step):
    i = pl.program_id(0)
    tm = pr_ref.shape[0]
    pr = pr_ref[...]
    if single_step:
        prev = prev_ref[...]
    else:
        rolled = pltpu.roll(pr, 1, 0)
        first = jnp.where((i * tm) % seq == 0, 0.0, 1.0) * prev_ref[7:8, :]
        row = lax.broadcasted_iota(jnp.int32, pr.shape, 0)
        prev = jnp.where(row == 0, first, rolled)
    ps = pr + mu_ref[...] * (prev - pr)
    r = ps[:, 0:c_r]
    kr = ps[:, c_r:2 * c_r]
    vr = ps[:, 2 * c_r:3 * c_r]
    wd = ps[:, 3 * c_r:3 * c_r + LANE]
    ad = ps[:, 3 * c_r + LANE:3 * c_r + 2 * LANE]
    gd = ps[:, 3 * c_r + 2 * LANE:3 * c_r + 3 * LANE]
    z = -(w0_ref[...] + jnp.dot(jnp.tanh(wd).astype(BF16), w2_ref[...], preferred_element_type=F32))
    softplus = jnp.maximum(z, 0.0) + jnp.log(1.0 + jnp.exp(-jnp.abs(z)))
    w_log = -softplus - 0.5
    a = _sigmoid(a0_ref[...] + jnp.dot(ad.astype(BF16), a2_ref[...], preferred_element_type=F32))
    g = jnp.dot(_sigmoid(gd).astype(BF16), g2_ref[...], preferred_element_type=F32)
    o_ref[0] = r
    o_ref[1] = kr * (1.0 + (a - 1.0) * kap_ref[...])
    o_ref[2] = vr
    o_ref[3] = -jnp.exp(w_log)
    o_ref[4] = kr * kkp_ref[...]
    o_ref[5] = a
    o_ref[6] = g


def rwkv_premix(proj, r_col0, prev, p, seq, single_step):
    m = proj.shape[0]
    c_r = p["w0"].shape[1]
    wr = 3 * c_r + 4 * LANE
    cb = r_col0 // wr
    tm = _pick(math.gcd(m, seq) if not single_step else m, (256, 128, 64, 32, 16, 8))
    if single_step:
        prev_spec = pl.BlockSpec((tm, wr), lambda i: (i, 0))
        prev_arg = prev
    else:
        prev_spec = pl.BlockSpec((8, wr), lambda i: (jnp.maximum(i * (tm // 8) - 1, 0), cb))
        prev_arg = proj
    vec = lambda n: pl.BlockSpec((1, n), lambda i: (0, 0))
    mat = lambda: pl.BlockSpec((LANE, c_r), lambda i: (0, 0))
    return pl.pallas_call(
        functools.partial(_rwkv_premix_kernel, c_r=c_r, seq=seq, single_step=single_step),
        grid=(m // tm,),
        in_specs=[pl.BlockSpec((tm, wr), lambda i: (i, cb)), prev_spec, vec(wr),
                  vec(c_r), mat(), vec(c_r), mat(), mat(), vec(c_r), vec(c_r)],
        out_specs=pl.BlockSpec((7, tm, c_r), lambda i: (0, i, 0)),
        out_shape=jax.ShapeDtypeStruct((7, m, c_r), F32),
        compiler_params=_cparams("parallel"),
        name="rwkv_premix",
    )(proj, prev_arg, p["mu"], p["w0"], p["w2"], p["a0"], p["a2"], p["g2"], p["kk"], p["ka"])


def _split3(x):
    hi = x.astype(BF16)
    r1 = x - hi.astype(F32)
    mid = r1.astype(BF16)
    lo = (r1 - mid.astype(F32)).astype(BF16)
    return hi, mid, lo


def _halfsum(x, m0):
    s0 = jnp.sum(jnp.where(m0, x, 0.0), axis=-1, keepdims=True)
    s1 = jnp.sum(jnp.where(m0, 0.0, x), axis=-1, keepdims=True)
    return jnp.where(m0, s0, s1)


def _bdot(a, b):
    return jnp.dot(a.astype(BF16), b.astype(BF16), preferred_element_type=F32)


def _bdot_nt(a, b):
    return lax.dot_general(a.astype(BF16), b.astype(BF16), (((1,), (1,)), ((), ())),
                           preferred_element_type=F32)


def _bdot_tn(a, b):
    return lax.dot_general(a.astype(BF16), b.astype(BF16), (((0,), (0,)), ((), ())),
                           preferred_element_type=F32)


def _same_head_mask():
    vrow = lax.broadcasted_iota(jnp.int32, (LANE, LANE), 0)
    kcol = lax.broadcasted_iota(jnp.int32, (LANE, LANE), 1)
    return (vrow < RWKV_HEAD) == (kcol < RWKV_HEAD)


def _rwkv_chunk_kernel(x_ref, rk_ref, rt2_ref, yvgb_ref, m2_ref, z_ref, ec_ref):
    C = x_ref.shape[1]
    N = RWKV_HEAD
    pairs = range(x_ref.shape[2] // LANE)
    sls = [slice(j * LANE, (j + 1) * LANE) for j in pairs]
    m0 = lax.broadcasted_iota(jnp.int32, (C, LANE), 1) < N
    m1 = jnp.logical_not(m0)
    trow = lax.broadcasted_iota(jnp.int32, (C, C), 0)
    tcol = lax.broadcasted_iota(jnp.int32, (C, C), 1)
    strict = tcol < trow
    incl = tcol <= trow
    tri = jnp.where(incl, 1.0, 0.0).astype(BF16)
    zeros = jnp.zeros((C, LANE), F32)
    same_head = _same_head_mask()

    lw = [x_ref[3, :, sl] for sl in sls]
    split = [_split3(x) for x in lw]
    cs = [jnp.dot(tri, s[0], preferred_element_type=F32) + jnp.dot(tri, s[1], preferred_element_type=F32)
          + jnp.dot(tri, s[2], preferred_element_type=F32) for s in split]

    r = [x_ref[0, :, sl] for sl in sls]
    k = [x_ref[1, :, sl] for sl in sls]
    v = [x_ref[2, :, sl] for sl in sls]
    a = [x_ref[5, :, sl] for sl in sls]
    kk = []
    for sl in sls:
        kr = x_ref[4, :, sl]
        kk.append(kr * lax.rsqrt(_halfsum(kr * kr, m0) + 1e-12))
    cs_end = [c[C - 1:C, :] for c in cs]
    al, rt, b_end, k_end, amat = [], [], [], [], []
    for j in pairs:
        e_neg = jnp.exp(-cs[j])
        al_j = -kk[j] * jnp.exp(cs[j] - lw[j])
        rt_j = r[j] * jnp.exp(cs[j])
        kb = kk[j] * a[j]
        e_end = jnp.exp(cs_end[j] - cs[j])
        al.append(al_j)
        rt.append(rt_j)
        b_end.append(kb * e_end)
        k_end.append(k[j] * e_end)
        amat.append(_bdot_nt(
            jnp.concatenate([jnp.where(m0, al_j, 0.0), jnp.where(m0, rt_j, 0.0),
                             jnp.where(m1, al_j, 0.0), jnp.where(m1, rt_j, 0.0)], axis=0),
            jnp.concatenate([kb * e_neg, k[j] * e_neg], axis=0)))

    heads = [(j, h) for j in pairs for h in range(2)]
    a_ab, a_rbk, x = {}, {}, {}
    for j, h in heads:
        base = 2 * C * h
        A = amat[j]
        a_ab[j, h] = jnp.where(strict, A[base:base + C, 0:C], 0.0)
        a_ak = jnp.where(strict, A[base:base + C, C:2 * C], 0.0)
        a_rbk[j, h] = jnp.concatenate([jnp.where(incl, A[base + C:base + 2 * C, 0:C], 0.0),
                                       jnp.where(incl, A[base + C:base + 2 * C, C:2 * C], 0.0)], axis=1)
        x[j, h] = jnp.concatenate([jnp.where(m0 if h == 0 else m1, al[j], 0.0), _bdot(a_ak, v[j])], axis=1)
    n = 1
    while n < C:
        for jh in heads:
            x[jh] = x[jh] + _bdot(a_ab[jh], x[jh])
        n *= 2
        if n < C:
            for jh in heads:
                a_ab[jh] = _bdot(a_ab[jh], a_ab[jh])

    uv = [jnp.where(m0, x[j, 0][:, LANE:], x[j, 1][:, LANE:]) for j in pairs]
    yh = {}
    for j, h in heads:
        yh[j, h] = _bdot(a_rbk[j, h],
                         jnp.concatenate([jnp.concatenate([x[j, h][:, :LANE], uv[j]], axis=1),
                                          jnp.concatenate([zeros, v[j]], axis=1)], axis=0))
    for j in pairs:
        sl = sls[j]
        al2 = x[j, 0][:, :LANE] + x[j, 1][:, :LANE]
        m2 = jnp.where(same_head, _bdot_tn(al2, b_end[j]), 0.0)
        z = jnp.where(same_head, _bdot_tn(jnp.concatenate([uv[j], v[j]], axis=0),
                                          jnp.concatenate([b_end[j], k_end[j]], axis=0)), 0.0)
        rt2 = rt[j] + yh[j, 0][:, :LANE] + yh[j, 1][:, :LANE]
        yv = jnp.where(m0, yh[j, 0][:, LANE:], yh[j, 1][:, LANE:])
        rt2_ref[:, sl] = rt2.astype(BF16)
        yvgb_ref[0, :, sl] = yv
        yvgb_ref[1, :, sl] = x_ref[6, :, sl]
        yvgb_ref[2, :, sl] = _halfsum(r[j] * k[j] * rk_ref[:, sl], m0) * v[j]
        m2_ref[j] = m2.astype(BF16)
        z_ref[j] = z
        ec_ref[:, sl] = jnp.exp(cs_end[j])


def _rwkv_state_kernel(rt2_ref, yvgb_ref, m2_ref, z_ref, ec_ref, lng_ref, lnb_ref, o_ref, s_out_ref, s_ref):
    c = pl.program_id(1)
    C = rt2_ref.shape[0]
    N = RWKV_HEAD

    @pl.when(c == 0)
    def _():
        s_ref[...] = jnp.zeros(s_ref.shape, F32)

    m0 = lax.broadcasted_iota(jnp.int32, (C, LANE), 1) < N
    for j in range(s_ref.shape[0]):
        sl = slice(j * LANE, (j + 1) * LANE)
        S = s_ref[j]
        Sb = S.astype(BF16)
        y = lax.dot_general(rt2_ref[:, sl], Sb, (((1,), (1,)), ((), ())),
                            preferred_element_type=F32) + yvgb_ref[0, :, sl]
        s_new = S * ec_ref[:, sl] + jnp.dot(Sb, m2_ref[j], preferred_element_type=F32) + z_ref[j]
        s_ref[j] = s_new
        s_out_ref[j] = s_new
        mu = _halfsum(y, m0) * (1.0 / N)
        d = y - mu
        var = _halfsum(d * d, m0) * (1.0 / N)
        yn = d * lax.rsqrt(var + LN_X_EPS) * lng_ref[:, sl] + lnb_ref[:, sl]
        o_ref[:, sl] = ((yn + yvgb_ref[2, :, sl]) * yvgb_ref[1, :, sl]).astype(o_ref.dtype)


def rwkv_scan(x7, rk, lnx_g, lnx_b, batch, seq):
    m, c_r = x7.shape[1], x7.shape[2]
    C = RWKV_CHUNK
    nc = seq // C
    nct = m // C
    npair = c_r // LANE
    pp = _pick(npair, (RWKV_PAIRS_PER_STEP, 2, 1))
    wl = pp * LANE
    rt2, yvgb, m2, z, ec = pl.pallas_call(
        _rwkv_chunk_kernel,
        grid=(nct, npair // pp),
        in_specs=[pl.BlockSpec((7, C, wl), lambda i, j: (0, i, j)),
                  pl.BlockSpec((1, wl), lambda i, j: (0, j))],
        out_specs=[pl.BlockSpec((C, wl), lambda i, j: (i, j)),
                   pl.BlockSpec((3, C, wl), lambda i, j: (0, i, j)),
                   pl.BlockSpec((None, pp, LANE, LANE), lambda i, j: (i, j, 0, 0)),
                   pl.BlockSpec((None, pp, LANE, LANE), lambda i, j: (i, j, 0, 0)),
                   pl.BlockSpec((None, 1, wl), lambda i, j: (i, 0, j))],
        out_shape=[jax.ShapeDtypeStruct((m, c_r), BF16),
                   jax.ShapeDtypeStruct((3, m, c_r), F32),
                   jax.ShapeDtypeStruct((nct, npair, LANE, LANE), BF16),
                   jax.ShapeDtypeStruct((nct, npair, LANE, LANE), F32),
                   jax.ShapeDtypeStruct((nct, 1, c_r), F32)],
        compiler_params=_cparams("parallel", "parallel"),
        name="rwkv_chunk",
    )(x7, rk)
    vec = lambda: pl.BlockSpec((1, c_r), lambda b, c: (0, 0))
    return pl.pallas_call(
        _rwkv_state_kernel,
        grid=(batch, nc),
        in_specs=[pl.BlockSpec((C, c_r), lambda b, c: (b * nc + c, 0)),
                  pl.BlockSpec((3, C, c_r), lambda b, c: (0, b * nc + c, 0)),
                  pl.BlockSpec((None, npair, LANE, LANE), lambda b, c: (b * nc + c, 0, 0, 0)),
                  pl.BlockSpec((None, npair, LANE, LANE), lambda b, c: (b * nc + c, 0, 0, 0)),
                  pl.BlockSpec((None, 1, c_r), lambda b, c: (b * nc + c, 0, 0)),
                  vec(), vec()],
        out_specs=[pl.BlockSpec((C, c_r), lambda b, c: (b * nc + c, 0)),
                   pl.BlockSpec((None, npair, LANE, LANE), lambda b, c: (b, 0, 0, 0))],
        out_shape=[jax.ShapeDtypeStruct((m, c_r), BF16),
                   jax.ShapeDtypeStruct((batch, npair, LANE, LANE), F32)],
        scratch_shapes=[pltpu.VMEM((npair, LANE, LANE), F32)],
        compiler_params=_cparams("parallel", "arbitrary"),
        name="rwkv_state",
    )(rt2, yvgb, m2, z, ec, lnx_g, lnx_b)


def _rwkv_step_kernel(s_ref, x_ref, rk_ref, lng_ref, lnb_ref, s_out_ref, o_ref):
    N = RWKV_HEAD
    S = s_ref[...]
    r = x_ref[0]
    k = x_ref[1]
    v = x_ref[2]
    w = jnp.exp(x_ref[3])
    kk = x_ref[4]
    a = x_ref[5]
    g = x_ref[6]
    kk = kk * lax.rsqrt(jnp.sum(kk * kk, axis=-1, keepdims=True) + 1e-12)
    eye = (lax.broadcasted_iota(jnp.int32, (1, N, N), 1) == lax.broadcasted_iota(jnp.int32, (1, N, N), 2))
    to_col = lambda row: jnp.sum(jnp.where(eye, row, 0.0), axis=-1, keepdims=True)
    to_row = lambda col: jnp.sum(jnp.where(eye, col, 0.0), axis=-2, keepdims=True)
    sk = -jnp.sum(S * kk, axis=-1, keepdims=True)
    s_new = S * w + sk * (kk * a) + to_col(v) * k
    s_out_ref[...] = s_new
    y = to_row(jnp.sum(s_new * r, axis=-1, keepdims=True))
    mu = jnp.mean(y, axis=-1, keepdims=True)
    d = y - mu
    var = jnp.mean(d * d, axis=-1, keepdims=True)
    yn = d * lax.rsqrt(var + LN_X_EPS) * lng_ref[...] + lnb_ref[...]
    bonus = jnp.sum(r * k * rk_ref[...], axis=-1, keepdims=True) * v
    o_ref[...] = (yn + bonus) * g


def rwkv_step(state, layer, x7, rk, lnx_g, lnx_b):
    _, bsz, nh, N, _ = state.shape
    bb = _pick(bsz, (16, 8, 4, 2, 1))
    x5 = x7.reshape(7, bsz, nh, 1, N)
    pv = lambda t: t.reshape(nh, 1, N)
    vec = lambda: pl.BlockSpec((None, 1, N), lambda i, h: (h, 0, 0))
    s_new, o = pl.pallas_call(
        _rwkv_step_kernel,
        grid=(bsz // bb, nh),
        in_specs=[pl.BlockSpec((None, bb, None, N, N), lambda i, h: (layer, i, h, 0, 0)),
                  pl.BlockSpec((7, bb, None, 1, N), lambda i, h: (0, i, h, 0, 0)),
                  vec(), vec(), vec()],
        out_specs=[pl.BlockSpec((bb, None, N, N), lambda i, h: (i, h, 0, 0)),
                   pl.BlockSpec((bb, None, 1, N), lambda i, h: (i, h, 0, 0))],
        out_shape=[jax.ShapeDtypeStruct(state.shape[1:], F32),
                   jax.ShapeDtypeStruct((bsz, nh, 1, N), F32)],
        compiler_params=_cparams("parallel", "parallel"),
        name="rwkv_step",
    )(state, x5, pv(rk), pv(lnx_g), pv(lnx_b))
    return s_new, o.reshape(bsz, nh * N)


def _ln_silu(y, g, b):
    mu = jnp.mean(y, axis=-1, keepdims=True)
    d = y - mu
    var = jnp.mean(d * d, axis=-1, keepdims=True)
    z = d * lax.rsqrt(var + LN_EPS) * g + b
    return z * _sigmoid(z)


def _conv_seq_kernel(cu_ref, halo_ref, w_ref, b_ref, lg_ref, lb_ref, o_ref, tail_ref, full_ref, *, seq):
    i = pl.program_id(0)
    tm = cu_ref.shape[0]
    c = cu_ref.shape[1] // 2
    glu = lambda ref: ref[:, 0:c] * _sigmoid(ref[:, c:2 * c])
    keep = jnp.where((i * tm) % seq == 0, 0.0, 1.0)
    full_ref[0:CONV_HALO, :] = glu(halo_ref) * keep
    full_ref[CONV_HALO:CONV_HALO + tm, :] = glu(cu_ref)
    acc = jnp.zeros((tm, c), F32) + b_ref[...]
    off = CONV_HALO - (CONV_W - 1)
    for j in range(CONV_W):
        acc = acc + w_ref[j:j + 1, :] * full_ref[off + j:off + j + tm, :]
    o_ref[...] = _ln_silu(acc, lg_ref[...], lb_ref[...]).astype(o_ref.dtype)
    tail_ref[...] = full_ref[tm:tm + CONV_HALO, :]


def conv_seq(proj, c_col0, conv_w, conv_b, ln_g, ln_b, batch, seq):
    m = proj.shape[0]
    c = conv_w.shape[1]
    tm = _pick(seq, (256, 128, 64, 32))
    cb = c_col0 // (2 * c)
    hb = tm // CONV_HALO
    vec = lambda: pl.BlockSpec((1, c), lambda i: (0, 0))
    return pl.pallas_call(
        functools.partial(_conv_seq_kernel, seq=seq),
        grid=(m // tm,),
        in_specs=[pl.BlockSpec((tm, 2 * c), lambda i: (i, cb)),
                  pl.BlockSpec((CONV_HALO, 2 * c), lambda i: (jnp.maximum(i * hb - 1, 0), cb)),
                  pl.BlockSpec((CONV_W, c), lambda i: (0, 0)), vec(), vec(), vec()],
        out_specs=[pl.BlockSpec((tm, c), lambda i: (i, 0)),
                   pl.BlockSpec((None, CONV_HALO, c), lambda i: ((i * tm) // seq, 0, 0))],
        out_shape=[jax.ShapeDtypeStruct((m, c), BF16),
                   jax.ShapeDtypeStruct((batch, CONV_HALO, c), F32)],
        scratch_shapes=[pltpu.VMEM((CONV_HALO + tm, c), F32)],
        compiler_params=_cparams("arbitrary"),
        name="conv_seq",
    )(proj, proj, conv_w, conv_b.reshape(1, c), ln_g.reshape(1, c), ln_b.reshape(1, c))


def _conv_step_kernel(cu_ref, cp_ref, w_ref, b_ref, lg_ref, lb_ref, o_ref, cn_ref):
    c = cu_ref.shape[1] // 2
    u = cu_ref[:, 0:c] * _sigmoid(cu_ref[:, c:2 * c])
    acc = b_ref[...] + w_ref[CONV_W - 1:CONV_W, :] * u
    for j in range(CONV_W - 1):
        acc = acc + w_ref[j:j + 1, :] * cp_ref[:, j, :]
    o_ref[...] = _ln_silu(acc, lg_ref[...], lb_ref[...]).astype(o_ref.dtype)
    cn_ref[:, 0:CONV_W - 2, :] = cp_ref[:, 1:CONV_W - 1, :]
    cn_ref[:, CONV_W - 2, :] = u


def conv_step(proj, c_col0, conv_prev, layer, conv_w, conv_b, ln_g, ln_b):
    bsz = proj.shape[0]
    c = conv_w.shape[1]
    bb = _pick(bsz, (8,))
    cb = c_col0 // (2 * c)
    vec = lambda: pl.BlockSpec((1, c), lambda i: (0, 0))
    return pl.pallas_call(
        _conv_step_kernel,
        grid=(bsz // bb,),
        in_specs=[pl.BlockSpec((bb, 2 * c), lambda i: (i, cb)),
                  pl.BlockSpec((None, bb, CONV_W - 1, c), lambda i: (layer, i, 0, 0)),
                  pl.BlockSpec((CONV_W, c), lambda i: (0, 0)), vec(), vec(), vec()],
        out_specs=[pl.BlockSpec((bb, c), lambda i: (i, 0)),
                   pl.BlockSpec((bb, CONV_W - 1, c), lambda i: (i, 0, 0))],
        out_shape=[jax.ShapeDtypeStruct((bsz, c), BF16),
                   jax.ShapeDtypeStruct((bsz, CONV_W - 1, c), F32)],
        compiler_params=_cparams("parallel"),
        name="conv_step",
    )(proj, conv_prev, conv_w, conv_b.reshape(1, c), ln_g.reshape(1, c), ln_b.reshape(1, c))


def _merge_kernel(xa_ref, xb_ref, xc_ref, wa_ref, wb_ref, wc_ref, ga_ref, gb_ref, gc_ref, o_ref):
    m = _sigmoid(ga_ref[...]) * jnp.dot(xa_ref[...], wa_ref[...], preferred_element_type=F32)
    m = m + _sigmoid(gb_ref[...]) * jnp.dot(xb_ref[...], wb_ref[...], preferred_element_type=F32)
    m = m + _sigmoid(gc_ref[...]) * jnp.dot(xc_ref[...], wc_ref[...], preferred_element_type=F32)
    o_ref[...] = m.astype(o_ref.dtype)


def gated_merge(o_a, o_b, o_c, w_pa, w_pb, w_pc, layer, proj, g_col0):
    m = o_a.shape[0]
    d = w_pa.shape[-1]
    tm = _pick(m, (512, 256, 128, 64, 32, 16))
    tn = _pick(math.gcd(d, g_col0), (512, 256, 128))
    gb0 = g_col0 // tn
    nb = d // tn
    xs = lambda kd: pl.BlockSpec((tm, kd), lambda i, j: (i, 0))
    ws = lambda kd: pl.BlockSpec((None, kd, tn), lambda i, j: (layer, 0, j))
    gs = lambda br: pl.BlockSpec((tm, tn), lambda i, j: (i, gb0 + br * nb + j))
    return pl.pallas_call(
        _merge_kernel,
        grid=(m // tm, nb),
        in_specs=[xs(o_a.shape[1]), xs(o_b.shape[1]), xs(o_c.shape[1]),
                  ws(w_pa.shape[1]), ws(w_pb.shape[1]), ws(w_pc.shape[1]), gs(0), gs(1), gs(2)],
        out_specs=pl.BlockSpec((tm, tn), lambda i, j: (i, j)),
        out_shape=jax.ShapeDtypeStruct((m, d), BF16),
        compiler_params=_cparams("parallel", "parallel"),
        name="gated_merge",
    )(o_a, o_b, o_c, w_pa, w_pb, w_pc, proj, proj, proj)


def _rope_tables(pos):
    half = HEAD_DIM // 2
    freqs = ROPE_THETA ** (-jnp.arange(half, dtype=F32) * (2.0 / HEAD_DIM))
    ang = pos.astype(F32)[:, None] * freqs[None, :]
    cos, sin = jnp.cos(ang), jnp.sin(ang)
    return jnp.concatenate([cos, cos], axis=-1), jnp.concatenate([-sin, sin], axis=-1)


def _layer(x2d, batch, seq, lp, lam, lambda_init, tables, sample_state):
    d_model = x2d.shape[1]
    layer = lp["layer"]
    c_r = lp["w_pb"].shape[1]
    c_c = lp["w_pc"].shape[1]
    col_v = 2 * A_WIDTH
    col_g = 3 * A_WIDTH
    col_c = col_g + 3 * d_model
    col_r = col_c + 2 * c_c
    out_scale = 1.0 - lambda_init

    h = rmsnorm_bf16(x2d, lp["norm1_g"])
    proj = matmul(h, lp["w_in_cat"], F32)

    qk = qknorm_rope(proj, lp["qk_gain"], *tables)
    if sample_state is None:
        o_a = flash_diff_attention(lam, qk, proj, col_v, lp["subln_g"], batch, seq, out_scale)
    else:
        o_a = paged_diff_attention(lam, qk, proj, col_v, sample_state["cache_k"], sample_state["cache_v"],
                                   sample_state["layer"], sample_state["page_table"], lp["subln_g"], out_scale)
    new_k = qk[1]
    new_v = proj[:, col_v:col_v + A_WIDTH]

    last_rows = proj[seq - 1::seq, col_r:]
    if sample_state is None:
        x7 = rwkv_premix(proj, col_r, None, lp["rwkv"], seq, False)
        o_b, pair_state = rwkv_scan(x7, lp["rk"], lp["lnx_g"], lp["lnx_b"], batch, seq)
        nh = c_r // RWKV_HEAD
        ps = pair_state.reshape(batch, nh // 2, 2, RWKV_HEAD, 2, RWKV_HEAD)
        new_state = jnp.stack([ps[:, :, 0, :, 0, :], ps[:, :, 1, :, 1, :]], axis=2).reshape(
            batch, nh, RWKV_HEAD, RWKV_HEAD)
    else:
        x7 = rwkv_premix(proj, col_r, _rwkv_pad_cols(sample_state["shift"], c_r), lp["rwkv"], seq, True)
        new_state, o_b = rwkv_step(sample_state["rwkv"], layer, x7, lp["rk"], lp["lnx_g"], lp["lnx_b"])
        o_b = o_b.astype(BF16)
    new_shift = _rwkv_unpad_cols(last_rows, c_r)

    if sample_state is None:
        o_c, tail = conv_seq(proj, col_c, lp["conv_w"], lp["conv_b"], lp["conv_ln_g"], lp["conv_ln_b"], batch, seq)
        new_conv = tail[:, CONV_HALO - (CONV_W - 1):]
    else:
        o_c, new_conv = conv_step(proj, col_c, sample_state["conv"], layer, lp["conv_w"], lp["conv_b"],
                                  lp["conv_ln_g"], lp["conv_ln_b"])

    merged = gated_merge(o_a, o_b, o_c, lp["w_pa"], lp["w_pb"], lp["w_pc"], layer, proj, col_g)
    x2d = matmul(merged, lp["w_o"], F32, "residual", x2d, layer=layer)
    h2 = rmsnorm_bf16(x2d, lp["norm2_g"])
    ff = matmul(h2, lp["w_mlp1"], BF16, "relu2", layer=layer)
    x2d = matmul(ff, lp["w_mlp2"], F32, "residual", x2d, layer=layer)
    return x2d, new_k, new_v, new_state, new_shift, new_conv


def kernel(x_prompt, x_sample, cache_k, cache_v, page_table, state_rwkv, state_shift, state_conv, norm1_g, w_in, q_norm_g, k_norm_g, lambda_q1, lambda_k1, lambda_q2, lambda_k2, subln_g, rwkv_mu, rwkv_w0, rwkv_w2, rwkv_a0, rwkv_a2, rwkv_g2, rwkv_kk, rwkv_ka, rwkv_rk, rwkv_lnx_g, rwkv_lnx_b, conv_w, conv_b, conv_ln_g, conv_ln_b, w_pa, w_pb, w_pc, w_o, norm2_g, w_mlp1, w_mlp2):
    depth = w_in.shape[0]
    b_p, t_p, d_model = x_prompt.shape
    b_s, t_s, _ = x_sample.shape
    assert t_s == 1
    c_r = w_pb.shape[1]
    c_c = w_pc.shape[1]
    past_len = page_table.shape[1] * PAGE_SIZE
    o_r = 3 * A_WIDTH
    o_c = o_r + 3 * c_r + DECAY_RANK + A_RANK + G_RANK
    o_g = o_c + 2 * c_c

    tab_p = _rope_tables(jnp.arange(t_p))
    tab_s = _rope_tables(jnp.full((b_s,), past_len))

    pad_rows = lambda w, n: jnp.concatenate([w, jnp.zeros((n - w.shape[0], w.shape[1]), w.dtype)], axis=0)
    row = lambda v: v.reshape(1, -1)

    stacked = dict(w_pa=w_pa.astype(BF16), w_pb=w_pb.astype(BF16), w_pc=w_pc.astype(BF16),
                   w_o=w_o.astype(BF16), w_mlp1=w_mlp1.astype(BF16), w_mlp2=w_mlp2.astype(BF16))

    yp = x_prompt.reshape(b_p * t_p, d_model)
    ys = x_sample.reshape(b_s, d_model)
    outs = [[] for _ in range(10)]
    for l in range(depth):
        wl = w_in[l]
        lp = dict(
            norm1_g=norm1_g[l], norm2_g=norm2_g[l],
            w_in_cat=jnp.concatenate([wl[:, :o_r], wl[:, o_g:], wl[:, o_c:o_g],
                                      _rwkv_pad_cols(wl[:, o_r:o_c], c_r)], axis=1).astype(BF16),
            qk_gain=jnp.stack([q_norm_g[l], k_norm_g[l]]).reshape(2, 1, HEAD_DIM),
            subln_g=subln_g[l],
            rwkv=dict(mu=row(_rwkv_pad_cols(rwkv_mu[l], c_r)), w0=row(rwkv_w0[l]),
                      w2=pad_rows(rwkv_w2[l], LANE).astype(BF16), a0=row(rwkv_a0[l]),
                      a2=pad_rows(rwkv_a2[l], LANE).astype(BF16), g2=pad_rows(rwkv_g2[l], LANE).astype(BF16),
                      kk=row(rwkv_kk[l]), ka=row(rwkv_ka[l])),
            rk=row(rwkv_rk[l]), lnx_g=row(rwkv_lnx_g[l]), lnx_b=row(rwkv_lnx_b[l]),
            conv_w=conv_w[l], conv_b=conv_b[l], conv_ln_g=conv_ln_g[l], conv_ln_b=conv_ln_b[l],
            w_pa=stacked["w_pa"], w_pb=stacked["w_pb"], w_pc=stacked["w_pc"],
            w_o=stacked["w_o"], w_mlp1=stacked["w_mlp1"], w_mlp2=stacked["w_mlp2"], layer=l,
        )
        lambda_init = 0.8 - 0.6 * math.exp(-0.3 * l)
        lam = (jnp.exp(jnp.sum(lambda_q1[l] * lambda_k1[l])) - jnp.exp(jnp.sum(lambda_q2[l] * lambda_k2[l]))
               + lambda_init).reshape(1).astype(F32)

        yp, k_new, v_new, s_new, sh_new, c_new = _layer(yp, b_p, t_p, lp, lam, lambda_init, tab_p, None)
        outs[0].append(k_new.reshape(b_p, t_p, N_HEADS_A, 2, HEAD_DIM))
        outs[1].append(v_new.reshape(b_p, t_p, N_HEADS_A, 2 * HEAD_DIM))
        outs[4].append(s_new)
        outs[6].append(sh_new)
        outs[8].append(c_new)

        st = dict(cache_k=cache_k, cache_v=cache_v, layer=l, page_table=page_table, rwkv=state_rwkv,
                  shift=state_shift[l], conv=state_conv)
        ys, k_new, v_new, s_new, sh_new, c_new = _layer(ys, b_s, 1, lp, lam, lambda_init, tab_s, st)
        outs[2].append(k_new.reshape(b_s, 1, N_HEADS_A, 2, HEAD_DIM))
        outs[3].append(v_new.reshape(b_s, 1, N_HEADS_A, 2 * HEAD_DIM))
        outs[5].append(s_new)
        outs[7].append(sh_new)
        outs[9].append(c_new)
    return (yp.reshape(b_p, t_p, d_model), ys.reshape(b_s, 1, d_model)) + tuple(jnp.stack(o) for o in outs)
```
